```python
import math
import jax, jax.numpy as jnp
from jax import lax
import numpy as np

D_MODEL = 2048
BATCH = 4
SEQ = 2048
DEPTH = 2
DEC_BATCH = 128
DEC_SEQ = 4
PAST_LEN = 16384
PAGE_SIZE = 128

D_MIX = D_MODEL
D_CONV = D_MIX // 2
D_SSM = D_MIX - D_CONV
D_IN = 2 * D_CONV + D_SSM
CONV_WIDTH = 31
CONV_BUF = CONV_WIDTH - 1
SSM_GROUP = 16
N_SSM_GROUPS = D_SSM // SSM_GROUP
SSM_STATE = 64
DT_MIN = 1e-3
DT_MAX = 1e-1
D_FF = ((8 * D_MODEL // 3 + 255) // 256) * 256
N_EXPERTS = 8
TOP_K = 2
D_FF_EXPERT = 7 * D_MODEL // 2
N_DENSE = (DEPTH + 1) // 2
N_MOE = DEPTH // 2
EPS = 1e-6

kernel_name = 'hymba_style_conformer_conv_s5_adaln_moe_step'


def rmsnorm(x, g):
    xf = x.astype(jnp.float32)
    y = xf * lax.rsqrt(jnp.mean(xf * xf, axis=-1, keepdims=True) + EPS)
    return (y * g.astype(jnp.float32)).astype(x.dtype)


def layernorm(x, g, b):
    xf = x.astype(jnp.float32)
    xc = xf - jnp.mean(xf, axis=-1, keepdims=True)
    var = jnp.mean(xc * xc, axis=-1, keepdims=True)
    y = xc * lax.rsqrt(var + EPS) * g.astype(jnp.float32) + b.astype(jnp.float32)
    return y.astype(x.dtype)


def modulate(h, shift, scale):
    return h * (1 + scale[:, None, :]) + shift[:, None, :]


def swiglu(h, w_gate, w_up, w_down):
    return (jax.nn.silu(h @ w_gate) * (h @ w_up)) @ w_down


def conformer_conv(a, g, prev, w_dw, b_dw, ln_g, ln_b):
    v = a * jax.nn.sigmoid(g)
    vp = jnp.concatenate([prev.astype(v.dtype), v], axis=1)
    y = lax.conv_general_dilated(vp, w_dw[:, None, :].astype(v.dtype), window_strides=(1,),
                                 padding='VALID', dimension_numbers=('NWC', 'WIO', 'NWC'),
                                 feature_group_count=D_CONV)
    y = layernorm(y + b_dw, ln_g, ln_b)
    return jax.nn.silu(y), vp[:, -CONV_BUF:]


def _ssm_combine(left, right):
    a1r, a1i, b1r, b1i = left
    a2r, a2i, b2r, b2i = right
    return (a1r * a2r - a1i * a2i,
            a1r * a2i + a1i * a2r,
            a2r * b1r - a2i * b1i + b2r,
            a2r * b1i + a2i * b1r + b2i)


def s5_ssm(u, s0_re, s0_im, lam_re, lam_im, log_dt, b_re, b_im, c_re, c_im, d, w_glu, b_glu):
    f32 = jnp.float32
    bsz, t = u.shape[0], u.shape[1]
    dt = jnp.exp(log_dt.astype(f32))[:, None]
    lr = lam_re.astype(f32)
    li = lam_im.astype(f32)
    mag = jnp.exp(lr * dt)
    ar = mag * jnp.cos(li * dt)
    ai = mag * jnp.sin(li * dt)
    den = lr * lr + li * li
    cr = ((ar - 1) * lr + ai * li) / den
    ci = (ai * lr - (ar - 1) * li) / den
    br = b_re.astype(f32)
    bi = b_im.astype(f32)
    bbr = cr[..., None] * br - ci[..., None] * bi
    bbi = cr[..., None] * bi + ci[..., None] * br
    uf = u.astype(f32)
    ug = uf.reshape(bsz, t, N_SSM_GROUPS, SSM_GROUP)
    xr = jnp.einsum('btgi,gpi->btgp', ug, bbr)
    xi = jnp.einsum('btgi,gpi->btgp', ug, bbi)
    s0r = s0_re.astype(f32)
    s0i = s0_im.astype(f32)
    xr = xr.at[:, 0].add(ar * s0r - ai * s0i)
    xi = xi.at[:, 0].add(ar * s0i + ai * s0r)
    a_r = jnp.broadcast_to(ar, (1, t, N_SSM_GROUPS, SSM_STATE))
    a_i = jnp.broadcast_to(ai, (1, t, N_SSM_GROUPS, SSM_STATE))
    _, _, sr, si = lax.associative_scan(_ssm_combine, (a_r, a_i, xr, xi), axis=1)
    y = (jnp.einsum('btgp,gop->btgo', sr, c_re.astype(f32))
         - jnp.einsum('btgp,gop->btgo', si, c_im.astype(f32)))
    y = y.reshape(bsz, t, D_SSM) + d.astype(f32) * uf
    y = jax.nn.gelu(y)
    y = y * jax.nn.sigmoid(y @ w_glu.astype(f32) + b_glu.astype(f32))
    return y.astype(u.dtype), sr[:, -1], si[:, -1]


def moe_swiglu(h, w_router, b_router, w_gate, w_up, w_down):
    f32 = jnp.float32
    logits = jnp.einsum('btd,de->bte', h.astype(f32), w_router.astype(f32)) + b_router.astype(f32)
    top_val, top_idx = lax.top_k(logits, TOP_K)
    probs = jax.nn.softmax(top_val, axis=-1)
    comb = jnp.sum(jax.nn.one_hot(top_idx, N_EXPERTS, dtype=f32) * probs[..., None], axis=-2)
    out = jnp.zeros_like(h)
    for e in range(N_EXPERTS):
        out = out + comb[..., e:e + 1].astype(h.dtype) * swiglu(h, w_gate[e], w_up[e], w_down[e])
    return out


def trunk(x, c, conv_state, ssm_re, ssm_im, p):
    new_conv, new_re, new_im = [], [], []
    cs = jax.nn.silu(c)
    for l in range(DEPTH):
        mod = cs @ p['w_ada'][l] + p['b_ada'][l]
        sh_m, sc_m, gt_m, sh_f, sc_f, gt_f = jnp.split(mod, 6, axis=-1)
        h = modulate(rmsnorm(x, p['g_norm_mix'][l]), sh_m, sc_m)
        proj = h @ p['w_in'][l]
        a, g, u = jnp.split(proj, [D_CONV, 2 * D_CONV], axis=-1)
        yc, buf = conformer_conv(a, g, conv_state[l], p['w_dw'][l], p['b_dw'][l],
                                 p['g_ln_conv'][l], p['b_ln_conv'][l])
        ys, sr, si = s5_ssm(u, ssm_re[l], ssm_im[l], p['lam_re'][l], p['lam_im'][l], p['log_dt'][l],
                            p['b_ssm_re'][l], p['b_ssm_im'][l], p['c_ssm_re'][l], p['c_ssm_im'][l],
                            p['d_ssm'][l], p['w_glu'][l], p['b_glu'][l])
        x = x + gt_m[:, None, :] * (jnp.concatenate([yc, ys], axis=-1) @ p['w_out'][l])
        h = modulate(rmsnorm(x, p['g_norm_ffn'][l]), sh_f, sc_f)
        if l % 2 == 0:
            i = l // 2
            f = swiglu(h, p['w_gate_dense'][i], p['w_up_dense'][i], p['w_down_dense'][i])
        else:
            i = l // 2
            f = moe_swiglu(h, p['w_router'][i], p['b_router'][i], p['w_gate_exp'][i],
                           p['w_up_exp'][i], p['w_down_exp'][i])
        x = x + gt_f[:, None, :] * f
        new_conv.append(buf)
        new_re.append(sr)
        new_im.append(si)
    return x, jnp.stack(new_conv), jnp.stack(new_re), jnp.stack(new_im)


def setup_inputs(seed: int = 0) -> dict:
    key = jax.random.key(seed)
    ks = list(jax.random.split(key, 48))

    def nrm(shape, scale):
        return jax.random.normal(ks.pop(), shape, jnp.float32) * scale

    G, P, I = N_SSM_GROUPS, SSM_STATE, SSM_GROUP
    n_idx = jnp.arange(P, dtype=jnp.float32)
    return {
        'x_prompt': nrm((BATCH, SEQ, D_MODEL), 1.0),
        'x_sample': nrm((DEC_BATCH, DEC_SEQ, D_MODEL), 1.0),
        'c_prompt': nrm((BATCH, D_MODEL), 1.0),
        'c_sample': nrm((DEC_BATCH, D_MODEL), 1.0),
        'state_conv': nrm((DEPTH, DEC_BATCH, CONV_BUF, D_CONV), 1.0),
        'state_ssm_re': nrm((DEPTH, DEC_BATCH, G, P), 0.5),
        'state_ssm_im': nrm((DEPTH, DEC_BATCH, G, P), 0.5),
        'w_ada': nrm((DEPTH, D_MODEL, 6 * D_MODEL), 0.5 * D_MODEL ** -0.5),
        'b_ada': nrm((DEPTH, 6 * D_MODEL), 0.02),
        'g_norm_mix': 1.0 + nrm((DEPTH, D_MODEL), 0.05),
        'g_norm_ffn': 1.0 + nrm((DEPTH, D_MODEL), 0.05),
        'w_in': nrm((DEPTH, D_MODEL, D_IN), D_MODEL ** -0.5),
        'w_dw': nrm((DEPTH, CONV_WIDTH, D_CONV), CONV_WIDTH ** -0.5),
        'b_dw': nrm((DEPTH, D_CONV), 0.02),
        'g_ln_conv': 1.0 + nrm((DEPTH, D_CONV), 0.05),
        'b_ln_conv': nrm((DEPTH, D_CONV), 0.02),
        'lam_re': -0.5 + nrm((DEPTH, G, P), 0.01),
        'lam_im': math.pi * n_idx + nrm((DEPTH, G, P), 0.01),
        'log_dt': jax.random.uniform(ks.pop(), (DEPTH, G), jnp.float32,
                                     math.log(DT_MIN), math.log(DT_MAX)),
        'b_ssm_re': nrm((DEPTH, G, P, I), I ** -0.5),
        'b_ssm_im': nrm((DEPTH, G, P, I), I ** -0.5),
        'c_ssm_re': nrm((DEPTH, G, I, P), P ** -0.5),
        'c_ssm_im': nrm((DEPTH, G, I, P), P ** -0.5),
        'd_ssm': nrm((DEPTH, D_SSM), 1.0),
        'w_glu': nrm((DEPTH, D_SSM, D_SSM), D_SSM ** -0.5),
        'b_glu': nrm((DEPTH, D_SSM), 0.02),
        'w_out': nrm((DEPTH, D_MIX, D_MODEL), D_MIX ** -0.5),
        'w_gate_dense': nrm((N_DENSE, D_MODEL, D_FF), D_MODEL ** -0.5),
        'w_up_dense': nrm((N_DENSE, D_MODEL, D_FF), D_MODEL ** -0.5),
        'w_down_dense': nrm((N_DENSE, D_FF, D_MODEL), D_FF ** -0.5),
        'w_router': nrm((N_MOE, D_MODEL, N_EXPERTS), D_MODEL ** -0.5),
        'b_router': nrm((N_MOE, N_EXPERTS), 0.01),
        'w_gate_exp': nrm((N_MOE, N_EXPERTS, D_MODEL, D_FF_EXPERT), D_MODEL ** -0.5),
        'w_up_exp': nrm((N_MOE, N_EXPERTS, D_MODEL, D_FF_EXPERT), D_MODEL ** -0.5),
        'w_down_exp': nrm((N_MOE, N_EXPERTS, D_FF_EXPERT, D_MODEL), D_FF_EXPERT ** -0.5),
        'g_final': 1.0 + nrm((D_MODEL,), 0.05),
    }


def reference(x_prompt, x_sample, c_prompt, c_sample, state_conv, state_ssm_re, state_ssm_im,
              w_ada, b_ada, g_norm_mix, g_norm_ffn, w_in, w_dw, b_dw, g_ln_conv, b_ln_conv,
              lam_re, lam_im, log_dt, b_ssm_re, b_ssm_im, c_ssm_re, c_ssm_im, d_ssm, w_glu, b_glu,
              w_out, w_gate_dense, w_up_dense, w_down_dense, w_router, b_router,
              w_gate_exp, w_up_exp, w_down_exp, g_final):
    p = {'w_ada': w_ada, 'b_ada': b_ada, 'g_norm_mix': g_norm_mix, 'g_norm_ffn': g_norm_ffn,
         'w_in': w_in, 'w_dw': w_dw, 'b_dw': b_dw, 'g_ln_conv': g_ln_conv, 'b_ln_conv': b_ln_conv,
         'lam_re': lam_re, 'lam_im': lam_im, 'log_dt': log_dt, 'b_ssm_re': b_ssm_re,
         'b_ssm_im': b_ssm_im, 'c_ssm_re': c_ssm_re, 'c_ssm_im': c_ssm_im, 'd_ssm': d_ssm,
         'w_glu': w_glu, 'b_glu': b_glu, 'w_out': w_out, 'w_gate_dense': w_gate_dense,
         'w_up_dense': w_up_dense, 'w_down_dense': w_down_dense, 'w_router': w_router,
         'b_router': b_router, 'w_gate_exp': w_gate_exp, 'w_up_exp': w_up_exp,
         'w_down_exp': w_down_exp}
    bp = x_prompt.shape[0]
    zero_conv = jnp.zeros((DEPTH, bp, CONV_BUF, D_CONV), x_prompt.dtype)
    zero_ssm = jnp.zeros((DEPTH, bp, N_SSM_GROUPS, SSM_STATE), jnp.float32)
    h_p, conv_p, re_p, im_p = trunk(x_prompt, c_prompt, zero_conv, zero_ssm, zero_ssm, p)
    h_s, conv_s, re_s, im_s = trunk(x_sample, c_sample, state_conv, state_ssm_re, state_ssm_im, p)
    y_prompt = rmsnorm(h_p, g_final)
    y_sample = rmsnorm(h_s, g_final)
    return (y_prompt, y_sample, conv_p, re_p, im_p, conv_s, re_s, im_s)
```

```python
import functools

import jax
import jax.numpy as jnp
from jax import lax
from jax.experimental import pallas as pl
from jax.experimental.pallas import tpu as pltpu

F32 = jnp.float32
BF16 = jnp.bfloat16

D_MODEL = 2048
D_CONV = 1024
D_SSM = 1024
CONV_WIDTH = 31
CONV_BUF = CONV_WIDTH - 1
SSM_GROUP = 16
N_SSM_GROUPS = 64
SSM_STATE = 64
LOG2_SSM_GROUP = 4
LOG2_SSM_STATE = 6
N_STATE = N_SSM_GROUPS * SSM_STATE
N_EXPERTS = 8
EPS = 1e-6

S5_BLOCKS = 8
S5_BLOCK_IN = D_SSM // S5_BLOCKS
S5_BLOCK_STATE = N_STATE // S5_BLOCKS

LANES = 128
VMEM_LIMIT = 56 * 1024 * 1024

ROW_TILE = 512
MOE_ROW_TILE = 1024
MOE_FF_TILE = 256
FFN_FF_TILE = 512
CONV_TIME_TILE = 128
S5_TIME_TILE = 256

SHIFT_M, SCALE_M, GATE_M, SHIFT_F, SCALE_F, GATE_F = range(6)


def _params(semantics, vmem=VMEM_LIMIT):
    return pltpu.CompilerParams(dimension_semantics=semantics, vmem_limit_bytes=vmem)


def _dot(a, b):
    return jnp.dot(a, b, preferred_element_type=F32)


def _rmsnorm(x, g):
    return x * lax.rsqrt(jnp.mean(x * x, axis=-1, keepdims=True) + EPS) * g


def _layernorm(x, g, b):
    xc = x - jnp.mean(x, axis=-1, keepdims=True)
    var = jnp.mean(xc * xc, axis=-1, keepdims=True)
    return xc * lax.rsqrt(var + EPS) * g + b


def _cast_kernel(x_ref, o_ref):
    o_ref[...] = x_ref[...].astype(o_ref.dtype)


def _cast_bf16(x, rows):
    r, c = x.shape
    return pl.pallas_call(
        _cast_kernel,
        grid=(r // rows,),
        in_specs=[pl.BlockSpec((rows, c), lambda i: (i, 0))],
        out_specs=pl.BlockSpec((rows, c), lambda i: (i, 0)),
        out_shape=jax.ShapeDtypeStruct((r, c), BF16),
        compiler_params=_params(("parallel",)),
        name="cast_bf16",
    )(x)


def _ada_kernel(c_ref, w_ref, b_ref, o_ref):
    cs = jax.nn.silu(c_ref[...]).astype(BF16)
    o_ref[...] = _dot(cs, w_ref[...].astype(BF16)) + b_ref[...]


def _ada(c_all, w_ada, b_ada):
    depth, _, n = w_ada.shape
    rows = c_all.shape[0]
    tn = 1024
    return pl.pallas_call(
        _ada_kernel,
        grid=(depth, n // tn),
        in_specs=[
            pl.BlockSpec((rows, D_MODEL), lambda l, j: (0, 0)),
            pl.BlockSpec((None, D_MODEL, tn), lambda l, j: (l, 0, j)),
            pl.BlockSpec((None, 1, tn), lambda l, j: (l, 0, j)),
        ],
        out_specs=pl.BlockSpec((None, rows, tn), lambda l, j: (l, 0, j)),
        out_shape=jax.ShapeDtypeStruct((depth, rows, n), F32),
        compiler_params=_params(("parallel", "parallel")),
        name="ada_mod",
    )(c_all, w_ada, b_ada.reshape(depth, 1, n))


class _Group:
    def __init__(self, rows, rows_per_batch, mod, row_block_offset):
        self.rows = rows
        self.rows_per_batch = rows_per_batch
        self.mod = mod
        self.row_block_offset = row_block_offset

    def mod_spec(self, layer, chunk, tm, grid_rank):
        if self.rows_per_batch is None:
            def imap(*ids):
                return (layer, ids[0], chunk)
            return pl.BlockSpec((None, tm, D_MODEL), imap)
        per = self.rows_per_batch // tm

        def imap(*ids):
            return (layer, ids[0] // per, 0, chunk)
        return pl.BlockSpec((None, None, 1, D_MODEL), imap)


def _layer_in_kernel(*refs, has_prev):
    if has_prev:
        (x_ref, f_ref, gtf_ref, g_ref, sh_ref, sc_ref, wa_ref, wg_ref, wu_ref,
         xo_ref, v_ref, u_ref, h_scr) = refs
    else:
        (x_ref, g_ref, sh_ref, sc_ref, wa_ref, wg_ref, wu_ref, v_ref, u_ref, h_scr) = refs

    @pl.when(pl.program_id(1) == 0)
    def _():
        x = x_ref[...]
        if has_prev:
            x = x + gtf_ref[...] * f_ref[...]
            xo_ref[...] = x
        h = _rmsnorm(x, g_ref[...]) * (1 + sc_ref[...]) + sh_ref[...]
        h_scr[...] = h.astype(BF16)

    h = h_scr[...]
    a = _dot(h, wa_ref[...])
    g = _dot(h, wg_ref[...])
    v_ref[...] = a * jax.nn.sigmoid(g)
    u_ref[...] = _dot(h, wu_ref[...])


def _layer_in(grp, layer, x, f_prev, g_norm, w_in_bf16):
    tm, tn = ROW_TILE, 512
    nb = D_CONV // tn
    has_prev = f_prev is not None
    row = lambda i, j: (i, 0)
    in_specs = [pl.BlockSpec((tm, D_MODEL), row)]
    args = [x]
    if has_prev:
        off = grp.row_block_offset
        in_specs += [pl.BlockSpec((tm, D_MODEL), lambda i, j: (i + off, 0)),
                     grp.mod_spec(layer - 1, GATE_F, tm, 2)]
        args += [f_prev, grp.mod]
    in_specs += [
        pl.BlockSpec((1, D_MODEL), lambda i, j: (0, 0)),
        grp.mod_spec(layer, SHIFT_M, tm, 2),
        grp.mod_spec(layer, SCALE_M, tm, 2),
        pl.BlockSpec((D_MODEL, tn), lambda i, j: (0, j)),
        pl.BlockSpec((D_MODEL, tn), lambda i, j: (0, j + nb)),
        pl.BlockSpec((D_MODEL, tn), lambda i, j: (0, j + 2 * nb)),
    ]
    args += [g_norm, grp.mod, grp.mod, w_in_bf16, w_in_bf16, w_in_bf16]
    out_specs = [pl.BlockSpec((tm, tn), lambda i, j: (i, j)),
                 pl.BlockSpec((tm, tn), lambda i, j: (i, j))]
    out_shape = [jax.ShapeDtypeStruct((grp.rows, D_CONV), F32),
                 jax.ShapeDtypeStruct((grp.rows, D_SSM), F32)]
    if has_prev:
        out_specs = [pl.BlockSpec((tm, D_MODEL), row)] + out_specs
        out_shape = [jax.ShapeDtypeStruct((grp.rows, D_MODEL), F32)] + out_shape
    outs = pl.pallas_call(
        functools.partial(_layer_in_kernel, has_prev=has_prev),
        grid=(grp.rows // tm, nb),
        in_specs=in_specs,
        out_specs=out_specs,
        out_shape=out_shape,
        scratch_shapes=[pltpu.VMEM((tm, D_MODEL), BF16)],
        compiler_params=_params(("parallel", "arbitrary")),
        name="layer_in",
    )(*args)
    if has_prev:
        return outs
    return [x] + list(outs)


def _conv_epilogue(acc, b_ref, g_ref, be_ref):
    y = _layernorm(acc + b_ref[...], g_ref[...], be_ref[...])
    return jax.nn.silu(y).astype(BF16)


def _conv_prompt_kernel(v_ref, w_ref, b_ref, g_ref, be_ref, y_ref, st_ref, buf):
    tt = CONV_TIME_TILE
    t = pl.program_id(1)

    @pl.when(t == 0)
    def _():
        buf[0:32, :] = jnp.zeros((32, D_CONV), F32)

    @pl.when(t > 0)
    def _():
        buf[0:32, :] = buf[tt:tt + 32, :]

    buf[32:32 + tt, :] = v_ref[...]
    rc_rows, lc_lanes = 32, 256
    for rc in range(tt // rc_rows):
        cols = []
        for lc in range(D_CONV // lc_lanes):
            ls = slice(lc * lc_lanes, (lc + 1) * lc_lanes)
            acc = jnp.zeros((rc_rows, lc_lanes), F32)
            for k in range(CONV_WIDTH):
                r0 = rc * rc_rows + 2 + k
                acc = acc + w_ref[k:k + 1, ls] * buf[r0:r0 + rc_rows, ls]
            cols.append(acc)
        acc = jnp.concatenate(cols, axis=1)
        y_ref[rc * rc_rows:(rc + 1) * rc_rows, :] = _conv_epilogue(acc, b_ref, g_ref, be_ref)
    st_ref[...] = buf[32 + tt - CONV_BUF:32 + tt, :]


def _conv_prompt(v, w_dw, b_dw, ln_g, ln_b, batch, seq):
    tt = CONV_TIME_TILE
    nt = seq // tt
    vec = pl.BlockSpec((1, D_CONV), lambda b, t: (0, 0))
    return pl.pallas_call(
        _conv_prompt_kernel,
        grid=(batch, nt),
        in_specs=[
            pl.BlockSpec((tt, D_CONV), lambda b, t: (b * nt + t, 0)),
            pl.BlockSpec((CONV_WIDTH, D_CONV), lambda b, t: (0, 0)),
            vec, vec, vec,
        ],
        out_specs=[
            pl.BlockSpec((tt, D_CONV), lambda b, t: (b * nt + t, 0)),
            pl.BlockSpec((None, CONV_BUF, D_CONV), lambda b, t: (b, 0, 0)),
        ],
        out_shape=[jax.ShapeDtypeStruct((batch * seq, D_CONV), BF16),
                   jax.ShapeDtypeStruct((batch, CONV_BUF, D_CONV), F32)],
        scratch_shapes=[pltpu.VMEM((32 + tt, D_CONV), F32)],
        compiler_params=_params(("parallel", "arbitrary")),
        name="conv_prompt",
    )(v, w_dw, b_dw, ln_g, ln_b)


def _conv_sample_kernel(v_ref, st_ref, w_ref, b_ref, g_ref, be_ref, y_ref, nst_ref):
    steps = v_ref.shape[0]
    for t in range(steps):
        acc = None
        for k in range(CONV_WIDTH):
            j = t + k
            src = st_ref[j] if j < CONV_BUF else v_ref[j - CONV_BUF]
            term = w_ref[k:k + 1, :] * src
            acc = term if acc is None else acc + term
        y_ref[t] = _conv_epilogue(acc, b_ref, g_ref, be_ref)
    for j in range(CONV_BUF - steps):
        nst_ref[j] = st_ref[j + steps]
    for j in range(steps):
        nst_ref[CONV_BUF - steps + j] = v_ref[j]


def _conv_sample(v_tm, state_tm, w_dw, b_dw, ln_g, ln_b):
    steps, batch, _ = v_tm.shape
    bb = 32
    vec = pl.BlockSpec((1, D_CONV), lambda i: (0, 0))
    return pl.pallas_call(
        _conv_sample_kernel,
        grid=(batch // bb,),
        in_specs=[
            pl.BlockSpec((steps, bb, D_CONV), lambda i: (0, i, 0)),
            pl.BlockSpec((CONV_BUF, bb, D_CONV), lambda i: (0, i, 0)),
            pl.BlockSpec((CONV_WIDTH, D_CONV), lambda i: (0, 0)),
            vec, vec, vec,
        ],
        out_specs=[
            pl.BlockSpec((steps, bb, D_CONV), lambda i: (0, i, 0)),
            pl.BlockSpec((CONV_BUF, bb, D_CONV), lambda i: (0, i, 0)),
        ],
        out_shape=[jax.ShapeDtypeStruct((steps, batch, D_CONV), BF16),
                   jax.ShapeDtypeStruct((CONV_BUF, batch, D_CONV), F32)],
        compiler_params=_params(("parallel",)),
        name="conv_sample",
    )(v_tm, state_tm, w_dw, b_dw, ln_g, ln_b)


def _s5_params_kernel(lr_ref, li_ref, ld_ref, br_ref, bi_ref, cr_ref, ci_ref,
                      ar_ref, ai_ref, wb_ref, wc_ref):
    lr = lr_ref[...]
    li = li_ref[...]
    dt = jnp.exp(ld_ref[...])
    mag = jnp.exp(lr * dt)
    ar = mag * jnp.cos(li * dt)
    ai = mag * jnp.sin(li * dt)
    den = lr * lr + li * li
    cr = ((ar - 1) * lr + ai * li) / den
    ci = (ai * lr - (ar - 1) * li) / den
    ar_ref[...] = ar
    ai_ref[...] = ai

    br = br_ref[...]
    bi = bi_ref[...]
    shape_b = (S5_BLOCK_IN, S5_BLOCK_STATE)
    same_b = (jnp.right_shift(lax.broadcasted_iota(jnp.int32, shape_b, 0), LOG2_SSM_GROUP)
              == jnp.right_shift(lax.broadcasted_iota(jnp.int32, shape_b, 1), LOG2_SSM_STATE))
    wb_ref[:, 0:S5_BLOCK_STATE] = jnp.where(same_b, cr * br - ci * bi, 0.0).astype(BF16)
    wb_ref[:, S5_BLOCK_STATE:] = jnp.where(same_b, cr * bi + ci * br, 0.0).astype(BF16)

    shape_c = (S5_BLOCK_STATE, S5_BLOCK_IN)
    same_c = (jnp.right_shift(lax.broadcasted_iota(jnp.int32, shape_c, 0), LOG2_SSM_STATE)
              == jnp.right_shift(lax.broadcasted_iota(jnp.int32, shape_c, 1), LOG2_SSM_GROUP))
    wc_ref[0:S5_BLOCK_STATE, :] = jnp.where(same_c, cr_ref[...], 0.0).astype(BF16)
    wc_ref[S5_BLOCK_STATE:, :] = jnp.where(same_c, -ci_ref[...], 0.0).astype(BF16)


def _s5_params(lam_re, lam_im, log_dt, b_re, b_im, c_re, c_im):
    depth = lam_re.shape[0]
    nb, gpb = S5_BLOCKS, N_SSM_GROUPS // S5_BLOCKS

    def per_state(a):
        return a.reshape(depth, nb, 1, S5_BLOCK_STATE)

    ld = jnp.broadcast_to(log_dt[:, :, None], lam_re.shape)

    def b_layout(b):
        bt = jnp.transpose(b, (0, 1, 3, 2)).reshape(depth, nb, S5_BLOCK_IN, SSM_STATE)
        return jnp.tile(bt, (1, 1, 1, gpb))

    def c_layout(c):
        ct = c.reshape(depth, nb, gpb, SSM_GROUP, SSM_STATE)
        ct = jnp.transpose(ct, (0, 1, 4, 2, 3)).reshape(depth, nb, SSM_STATE, S5_BLOCK_IN)
        return jnp.tile(ct, (1, 1, gpb, 1))

    st = pl.BlockSpec((None, None, 1, S5_BLOCK_STATE), lambda l, r: (l, r, 0, 0))
    bs = pl.BlockSpec((None, None, S5_BLOCK_IN, S5_BLOCK_STATE), lambda l, r: (l, r, 0, 0))
    cs = pl.BlockSpec((None, None, S5_BLOCK_STATE, S5_BLOCK_IN), lambda l, r: (l, r, 0, 0))
    ar, ai, wb, wc = pl.pallas_call(
        _s5_params_kernel,
        grid=(depth, nb),
        in_specs=[st, st, st, bs, bs, cs, cs],
        out_specs=[
            st, st,
            pl.BlockSpec((None, None, S5_BLOCK_IN, 2 * S5_BLOCK_STATE), lambda l, r: (l, r, 0, 0)),
            pl.BlockSpec((None, None, 2 * S5_BLOCK_STATE, S5_BLOCK_IN), lambda l, r: (l, r, 0, 0)),
        ],
        out_shape=[
            jax.ShapeDtypeStruct((depth, nb, 1, S5_BLOCK_STATE), F32),
            jax.ShapeDtypeStruct((depth, nb, 1, S5_BLOCK_STATE), F32),
            jax.ShapeDtypeStruct((depth, nb, S5_BLOCK_IN, 2 * S5_BLOCK_STATE), BF16),
            jax.ShapeDtypeStruct((depth, nb, 2 * S5_BLOCK_STATE, S5_BLOCK_IN), BF16),
        ],
        compiler_params=_params(("parallel", "parallel")),
        name="s5_params",
    )(per_state(lam_re), per_state(lam_im), per_state(ld),
      b_layout(b_re), b_layout(b_im), c_layout(c_re), c_layout(c_im))
    return ar.reshape(depth, 1, N_STATE), ai.reshape(depth, 1, N_STATE), wb, wc


def _s5_output(y, u, d_ref, wglu_ref, bglu_ref):
    y = jax.nn.gelu(y + d_ref[...] * u)
    z = _dot(y.astype(BF16), wglu_ref[...]) + bglu_ref[...]
    return (y * jax.nn.sigmoid(z)).astype(BF16)


def _s5_prompt_kernel(u_ref, ar_ref, ai_ref, wb_ref, wc_ref, d_ref, wglu_ref, bglu_ref,
                      y_ref, sre_ref, sim_ref, xre, xim, cre, cim):
    ts = S5_TIME_TILE

    @pl.when(pl.program_id(1) == 0)
    def _():
        cre[...] = jnp.zeros_like(cre)
        cim[...] = jnp.zeros_like(cim)

    u = u_ref[...]
    ub = u.astype(BF16)
    for r in range(S5_BLOCKS):
        x = _dot(ub[:, r * S5_BLOCK_IN:(r + 1) * S5_BLOCK_IN], wb_ref[r])
        ls = slice(r * S5_BLOCK_STATE, (r + 1) * S5_BLOCK_STATE)
        xre[:, ls] = x[:, :S5_BLOCK_STATE]
        xim[:, ls] = x[:, S5_BLOCK_STATE:]

    for r in range(S5_BLOCKS):
        ls = slice(r * S5_BLOCK_STATE, (r + 1) * S5_BLOCK_STATE)
        a_r = ar_ref[:, ls]
        a_i = ai_ref[:, ls]

        def step(i, carry, ls=ls, a_r=a_r, a_i=a_i):
            sr, si = carry
            nr = a_r * sr - a_i * si + xre[pl.ds(i, 1), ls]
            ni = a_r * si + a_i * sr + xim[pl.ds(i, 1), ls]
            xre[pl.ds(i, 1), ls] = nr
            xim[pl.ds(i, 1), ls] = ni
            return nr, ni

        sr, si = lax.fori_loop(0, ts, step, (cre[:, ls], cim[:, ls]), unroll=8)
        cre[:, ls] = sr
        cim[:, ls] = si

    cols = []
    for r in range(S5_BLOCKS):
        ls = slice(r * S5_BLOCK_STATE, (r + 1) * S5_BLOCK_STATE)
        cols.append(_dot(xre[:, ls].astype(BF16), wc_ref[r, 0:S5_BLOCK_STATE, :])
                    + _dot(xim[:, ls].astype(BF16), wc_ref[r, S5_BLOCK_STATE:, :]))
    y_ref[...] = _s5_output(jnp.concatenate(cols, axis=1), u, d_ref, wglu_ref, bglu_ref)
    sre_ref[...] = cre[...]
    sim_ref[...] = cim[...]


def _s5_prompt(u, ar, ai, wb, wc, d, w_glu_bf16, b_glu, batch, seq):
    ts = S5_TIME_TILE
    nt = seq // ts
    full = lambda shape: pl.BlockSpec(shape, lambda b, t: (0,) * len(shape))
    y, sre, sim = pl.pallas_call(
        _s5_prompt_kernel,
        grid=(batch, nt),
        in_specs=[
            pl.BlockSpec((ts, D_SSM), lambda b, t: (b * nt + t, 0)),
            full((1, N_STATE)), full((1, N_STATE)),
            full(wb.shape), full(wc.shape),
            full((1, D_SSM)), full((D_SSM, D_SSM)), full((1, D_SSM)),
        ],
        out_specs=[
            pl.BlockSpec((ts, D_SSM), lambda b, t: (b * nt + t, 0)),
            pl.BlockSpec((None, 1, N_STATE), lambda b, t: (b, 0, 0)),
            pl.BlockSpec((None, 1, N_STATE), lambda b, t: (b, 0, 0)),
        ],
        out_shape=[jax.ShapeDtypeStruct((batch * seq, D_SSM), BF16),
                   jax.ShapeDtypeStruct((batch, 1, N_STATE), F32),
                   jax.ShapeDtypeStruct((batch, 1, N_STATE), F32)],
        scratch_shapes=[pltpu.VMEM((ts, N_STATE), F32), pltpu.VMEM((ts, N_STATE), F32),
                        pltpu.VMEM((1, N_STATE), F32), pltpu.VMEM((1, N_STATE), F32)],
        compiler_params=_params(("parallel", "arbitrary")),
        name="s5_prompt",
    )(u, ar, ai, wb, wc, d, w_glu_bf16, b_glu)
    return y, sre, sim


def _s5_sample_kernel(u_ref, s0r_ref, s0i_ref, ar_ref, ai_ref, wb_ref, wc_ref, d_ref,
                      wglu_ref, bglu_ref, y_ref, sre_ref, sim_ref, yscr):
    steps = u_ref.shape[0]
    for r in range(S5_BLOCKS):
        ls = slice(r * S5_BLOCK_STATE, (r + 1) * S5_BLOCK_STATE)
        cs = slice(r * S5_BLOCK_IN, (r + 1) * S5_BLOCK_IN)
        a_r = ar_ref[:, ls]
        a_i = ai_ref[:, ls]
        sr = s0r_ref[:, ls]
        si = s0i_ref[:, ls]
        for t in range(steps):
            x = _dot(u_ref[t, :, cs].astype(BF16), wb_ref[r])
            sr, si = (a_r * sr - a_i * si + x[:, :S5_BLOCK_STATE],
                      a_r * si + a_i * sr + x[:, S5_BLOCK_STATE:])
            yscr[t, :, cs] = (_dot(sr.astype(BF16), wc_ref[r, 0:S5_BLOCK_STATE, :])
                              + _dot(si.astype(BF16), wc_ref[r, S5_BLOCK_STATE:, :]))
        sre_ref[:, ls] = sr
        sim_ref[:, ls] = si
    for t in range(steps):
        y_ref[t] = _s5_output(yscr[t], u_ref[t], d_ref, wglu_ref, bglu_ref)


def _s5_sample(u_tm, s0r, s0i, ar, ai, wb, wc, d, w_glu_bf16, b_glu):
    steps, batch, _ = u_tm.shape
    return pl.pallas_call(
        _s5_sample_kernel,
        out_shape=[jax.ShapeDtypeStruct((steps, batch, D_SSM), BF16),
                   jax.ShapeDtypeStruct((batch, N_STATE), F32),
                   jax.ShapeDtypeStruct((batch, N_STATE), F32)],
        scratch_shapes=[pltpu.VMEM((steps, batch, D_SSM), F32)],
        compiler_params=pltpu.CompilerParams(vmem_limit_bytes=VMEM_LIMIT),
        name="s5_sample",
    )(u_tm, s0r, s0i, ar, ai, wb, wc, d, w_glu_bf16, b_glu)


def _layer_out_kernel(*refs, moe):
    if moe:
        (yc_ref, ys_ref, wo_ref, x_ref, gt_ref, g_ref, sh_ref, sc_ref, wr_ref, br_ref,
         xo_ref, h_ref, lg_ref) = refs
    else:
        (yc_ref, ys_ref, wo_ref, x_ref, gt_ref, g_ref, sh_ref, sc_ref, xo_ref, h_ref) = refs
    o = _dot(yc_ref[...], wo_ref[0:D_CONV, :]) + _dot(ys_ref[...], wo_ref[D_CONV:, :])
    x = x_ref[...] + gt_ref[...] * o
    xo_ref[...] = x
    h = _rmsnorm(x, g_ref[...]) * (1 + sc_ref[...]) + sh_ref[...]
    if moe:
        h_ref[...] = h
        lg_ref[...] = jnp.dot(h, wr_ref[...], precision=lax.Precision.HIGHEST,
                              preferred_element_type=F32) + br_ref[...]
    else:
        h_ref[...] = h.astype(BF16)


def _layer_out(grp, layer, yc, ys, w_out_bf16, x, g_norm, router=None):
    tm = 256
    moe = router is not None
    row = lambda i: (i, 0)
    const = lambda i: (0, 0)
    in_specs = [
        pl.BlockSpec((tm, D_CONV), row), pl.BlockSpec((tm, D_SSM), row),
        pl.BlockSpec((D_MODEL, D_MODEL), const),
        pl.BlockSpec((tm, D_MODEL), row),
        grp.mod_spec(layer, GATE_M, tm, 1),
        pl.BlockSpec((1, D_MODEL), const),
        grp.mod_spec(layer, SHIFT_F, tm, 1),
        grp.mod_spec(layer, SCALE_F, tm, 1),
    ]
    args = [yc, ys, w_out_bf16, x, grp.mod, g_norm, grp.mod, grp.mod]
    out_specs = [pl.BlockSpec((tm, D_MODEL), row), pl.BlockSpec((tm, D_MODEL), row)]
    out_shape = [jax.ShapeDtypeStruct((grp.rows, D_MODEL), F32),
                 jax.ShapeDtypeStruct((grp.rows, D_MODEL), F32 if moe else BF16)]
    if moe:
        in_specs += [pl.BlockSpec((D_MODEL, LANES), const), pl.BlockSpec((1, LANES), const)]
        args += list(router)
        out_specs.append(pl.BlockSpec((tm, LANES), row))
        out_shape.append(jax.ShapeDtypeStruct((grp.rows, LANES), F32))
    return pl.pallas_call(
        functools.partial(_layer_out_kernel, moe=moe),
        grid=(grp.rows // tm,),
        in_specs=in_specs,
        out_specs=out_specs,
        out_shape=out_shape,
        compiler_params=_params(("parallel",)),
        name="layer_out",
    )(*args)


def _two_group_specs(tm, n_first, grid_rank):
    def first(*ids):
        return (jnp.minimum(ids[0], n_first - 1), 0)

    def second(*ids):
        return (jnp.maximum(ids[0] - n_first, 0), 0)
    return (pl.BlockSpec((tm, D_MODEL), first), pl.BlockSpec((tm, D_MODEL), second))


def _ffn_kernel(hp_ref, hs_ref, wg_ref, wu_ref, wd_ref, o_ref, h_scr, *, n_first):
    i = pl.program_id(0)
    j = pl.program_id(1)

    @pl.when(jnp.logical_and(j == 0, i < n_first))
    def _():
        h_scr[...] = hp_ref[...]

    @pl.when(jnp.logical_and(j == 0, i >= n_first))
    def _():
        h_scr[...] = hs_ref[...]

    h = h_scr[...]
    act = (jax.nn.silu(_dot(h, wg_ref[...])) * _dot(h, wu_ref[...])).astype(BF16)
    contrib = _dot(act, wd_ref[...])

    @pl.when(j == 0)
    def _():
        o_ref[...] = contrib

    @pl.when(j > 0)
    def _():
        o_ref[...] += contrib


def _ffn_dense(h_p, h_s, wg, wu, wd):
    tm, tf = ROW_TILE, FFN_FF_TILE
    n_first = h_p.shape[0] // tm
    n_tiles = n_first + h_s.shape[0] // tm
    d_ff = wg.shape[1]
    sp, ss = _two_group_specs(tm, n_first, 2)
    return pl.pallas_call(
        functools.partial(_ffn_kernel, n_first=n_first),
        grid=(n_tiles, d_ff // tf),
        in_specs=[sp, ss,
                  pl.BlockSpec((D_MODEL, tf), lambda i, j: (0, j)),
                  pl.BlockSpec((D_MODEL, tf), lambda i, j: (0, j)),
                  pl.BlockSpec((tf, D_MODEL), lambda i, j: (j, 0))],
        out_specs=pl.BlockSpec((tm, D_MODEL), lambda i, j: (i, 0)),
        out_shape=jax.ShapeDtypeStruct((n_tiles * tm, D_MODEL), F32),
        scratch_shapes=[pltpu.VMEM((tm, D_MODEL), BF16)],
        compiler_params=_params(("parallel", "arbitrary")),
        name="ffn_dense",
    )(h_p, h_s, wg, wu, wd)


R_E1, R_E2, R_P1, R_P2, R_RANK1, R_RANK2 = range(6)


def _route_kernel(lp_ref, ls_ref, meta_ref, cnt_ref, carry, *, n_first):
    i = pl.program_id(0)
    tm = meta_ref.shape[0]

    @pl.when(i == 0)
    def _():
        carry[...] = jnp.zeros_like(carry)

    lane = lax.broadcasted_iota(jnp.int32, (tm, LANES), 1).astype(F32)
    logits = jnp.where(i < n_first, lp_ref[...], ls_ref[...])
    logits = jnp.where(lane < N_EXPERTS, logits, -jnp.inf)
    m1 = jnp.max(logits, axis=-1, keepdims=True)
    e1 = jnp.min(jnp.where(logits == m1, lane, float(LANES)), axis=-1, keepdims=True)
    rest = jnp.where(lane == e1, -jnp.inf, logits)
    m2 = jnp.max(rest, axis=-1, keepdims=True)
    e2 = jnp.min(jnp.where(rest == m2, lane, float(LANES)), axis=-1, keepdims=True)
    x2 = jnp.exp(m2 - m1)
    den = 1.0 + x2
    p1 = 1.0 / den
    p2 = x2 / den

    hot1 = lane == e1
    hot2 = lane == e2
    hot = jnp.logical_or(hot1, hot2).astype(F32)
    rows = lax.broadcasted_iota(jnp.int32, (tm, tm), 0)
    cols = lax.broadcasted_iota(jnp.int32, (tm, tm), 1)
    earlier = (cols < rows).astype(BF16)
    rank = _dot(earlier, hot.astype(BF16)) + carry[0:1, :]
    rank1 = jnp.sum(jnp.where(hot1, rank, 0.0), axis=-1, keepdims=True)
    rank2 = jnp.sum(jnp.where(hot2, rank, 0.0), axis=-1, keepdims=True)
    total = carry[0:1, :] + jnp.sum(hot, axis=0, keepdims=True)
    carry[...] = jnp.broadcast_to(total, carry.shape)
    cnt_ref[...] = jnp.broadcast_to(total, cnt_ref.shape)

    meta = jnp.where(lane == R_E1, e1.astype(F32), 0.0)
    meta = jnp.where(lane == R_E2, e2.astype(F32), meta)
    meta = jnp.where(lane == R_P1, p1, meta)
    meta = jnp.where(lane == R_P2, p2, meta)
    meta = jnp.where(lane == R_RANK1, rank1, meta)
    meta = jnp.where(lane == R_RANK2, rank2, meta)
    meta_ref[...] = meta


def _route(lg_p, lg_s):
    tm = ROW_TILE
    n_first = lg_p.shape[0] // tm
    n_tiles = n_first + lg_s.shape[0] // tm
    first = lambda i: (jnp.minimum(i, n_first - 1), 0)
    second = lambda i: (jnp.maximum(i - n_first, 0), 0)
    return pl.pallas_call(
        functools.partial(_route_kernel, n_first=n_first),
        grid=(n_tiles,),
        in_specs=[pl.BlockSpec((tm, LANES), first), pl.BlockSpec((tm, LANES), second)],
        out_specs=[pl.BlockSpec((tm, LANES), lambda i: (i, 0)),
                   pl.BlockSpec((8, LANES), lambda i: (0, 0))],
        out_shape=[jax.ShapeDtypeStruct((n_tiles * tm, LANES), F32),
                   jax.ShapeDtypeStruct((8, LANES), F32)],
        scratch_shapes=[pltpu.VMEM((8, LANES), F32)],
        compiler_params=_params(("arbitrary",)),
        name="moe_route",
    )(lg_p, lg_s)


def _row_copies(n, start_fn):
    def issue(r, c):
        for cp in start_fn(r):
            cp.start()
        return c
    lax.fori_loop(0, n, issue, 0)

    def drain(r, c):
        for cp in start_fn(0):
            cp.wait()
        return c
    lax.fori_loop(0, n, drain, 0)


def _dispatch_kernel(p1_ref, p2_ref, hp_ref, hs_ref, xs_in_ref, xs_ref, sem, *, n_first):
    del xs_in_ref
    i = pl.program_id(0)
    tm = hp_ref.shape[0]
    base = i * tm

    def run(src_ref):
        def copies(r):
            row = src_ref.at[pl.ds(r, 1), :]
            return (pltpu.make_async_copy(row, xs_ref.at[pl.ds(p1_ref[base + r], 1), :], sem.at[0]),
                    pltpu.make_async_copy(row, xs_ref.at[pl.ds(p2_ref[base + r], 1), :], sem.at[1]))
        _row_copies(tm, copies)

    @pl.when(i < n_first)
    def _():
        run(hp_ref)

    @pl.when(i >= n_first)
    def _():
        run(hs_ref)


def _dispatch(pos1, pos2, h_p, h_s, n_slots):
    tm = ROW_TILE
    n_first = h_p.shape[0] // tm
    n_tiles = n_first + h_s.shape[0] // tm
    first = lambda i, p1, p2: (jnp.minimum(i, n_first - 1), 0)
    second = lambda i, p1, p2: (jnp.maximum(i - n_first, 0), 0)
    xs0 = jnp.zeros((n_slots, D_MODEL), F32)
    return pl.pallas_call(
        functools.partial(_dispatch_kernel, n_first=n_first),
        grid_spec=pltpu.PrefetchScalarGridSpec(
            num_scalar_prefetch=2,
            grid=(n_tiles,),
            in_specs=[pl.BlockSpec((tm, D_MODEL), first), pl.BlockSpec((tm, D_MODEL), second),
                      pl.BlockSpec(memory_space=pl.ANY)],
            out_specs=pl.BlockSpec(memory_space=pl.ANY),
            scratch_shapes=[pltpu.SemaphoreType.DMA((2,))],
        ),
        out_shape=jax.ShapeDtypeStruct((n_slots, D_MODEL), F32),
        input_output_aliases={4: 0},
        compiler_params=_params(("arbitrary",)),
        name="moe_dispatch",
    )(pos1, pos2, h_p, h_s, xs0)


def _moe_kernel(te_ref, nu_ref, xs_ref, wg_ref, wu_ref, wd_ref, o_ref, h_scr):
    i = pl.program_id(0)
    j = pl.program_id(1)

    @pl.when(j == 0)
    def _():
        o_ref[...] = jnp.zeros_like(o_ref)
        h_scr[...] = xs_ref[...].astype(BF16)

    @pl.when(i < nu_ref[0])
    def _():
        h = h_scr[...]
        g = _dot(h, wg_ref[...].astype(BF16))
        u = _dot(h, wu_ref[...].astype(BF16))
        act = (jax.nn.silu(g) * u).astype(BF16)
        o_ref[...] += _dot(act, wd_ref[...].astype(BF16))


def _moe_experts(tile_expert, n_used, xs, wg, wu, wd):
    tm, tf = MOE_ROW_TILE, MOE_FF_TILE
    n_tiles = xs.shape[0] // tm
    d_ff = wg.shape[2]
    nj = d_ff // tf

    def ff(i, j, te, nu):
        return jnp.where(i < nu[0], j, nj - 1)

    return pl.pallas_call(
        _moe_kernel,
        grid_spec=pltpu.PrefetchScalarGridSpec(
            num_scalar_prefetch=2,
            grid=(n_tiles, nj),
            in_specs=[
                pl.BlockSpec((tm, D_MODEL), lambda i, j, te, nu: (i, 0)),
                pl.BlockSpec((None, D_MODEL, tf), lambda i, j, te, nu: (te[i], 0, ff(i, j, te, nu))),
                pl.BlockSpec((None, D_MODEL, tf), lambda i, j, te, nu: (te[i], 0, ff(i, j, te, nu))),
                pl.BlockSpec((None, tf, D_MODEL), lambda i, j, te, nu: (te[i], ff(i, j, te, nu), 0)),
            ],
            out_specs=pl.BlockSpec((tm, D_MODEL), lambda i, j, te, nu: (i, 0)),
            scratch_shapes=[pltpu.VMEM((tm, D_MODEL), BF16)],
        ),
        out_shape=jax.ShapeDtypeStruct(xs.shape, F32),
        compiler_params=_params(("arbitrary", "arbitrary")),
        name="moe_experts",
    )(tile_expert, n_used, xs, wg, wu, wd)


def _combine_kernel(p1_ref, p2_ref, ys_ref, meta_ref, o_ref, y1, y2, sem):
    tm = o_ref.shape[0]
    base = pl.program_id(0) * tm

    def copies(r):
        return (pltpu.make_async_copy(ys_ref.at[pl.ds(p1_ref[base + r], 1), :],
                                      y1.at[pl.ds(r, 1), :], sem.at[0]),
                pltpu.make_async_copy(ys_ref.at[pl.ds(p2_ref[base + r], 1), :],
                                      y2.at[pl.ds(r, 1), :], sem.at[1]))
    _row_copies(tm, copies)
    meta = meta_ref[...]
    o_ref[...] = meta[:, R_P1:R_P1 + 1] * y1[...] + meta[:, R_P2:R_P2 + 1] * y2[...]


def _combine(pos1, pos2, ys, meta):
    tm = ROW_TILE
    n_rows = meta.shape[0]
    return pl.pallas_call(
        _combine_kernel,
        grid_spec=pltpu.PrefetchScalarGridSpec(
            num_scalar_prefetch=2,
            grid=(n_rows // tm,),
            in_specs=[pl.BlockSpec(memory_space=pl.ANY),
                      pl.BlockSpec((tm, LANES), lambda i, p1, p2: (i, 0))],
            out_specs=pl.BlockSpec((tm, D_MODEL), lambda i, p1, p2: (i, 0)),
            scratch_shapes=[pltpu.VMEM((tm, D_MODEL), F32), pltpu.VMEM((tm, D_MODEL), F32),
                            pltpu.SemaphoreType.DMA((2,))],
        ),
        out_shape=jax.ShapeDtypeStruct((n_rows, D_MODEL), F32),
        compiler_params=_params(("arbitrary",)),
        name="moe_combine",
    )(pos1, pos2, ys, meta)


def _moe(h_p, h_s, lg_p, lg_s, wg, wu, wd):
    tm = MOE_ROW_TILE
    n_rows = h_p.shape[0] + h_s.shape[0]
    max_tiles = (2 * n_rows) // tm + N_EXPERTS
    meta, cnt = _route(lg_p, lg_s)
    counts = cnt[0, :N_EXPERTS].astype(jnp.int32)
    tiles = (counts + tm - 1) // tm
    ends = jnp.cumsum(tiles)
    seg_start = (ends - tiles) * tm
    e1 = meta[:, R_E1].astype(jnp.int32)
    e2 = meta[:, R_E2].astype(jnp.int32)
    pos1 = seg_start[e1] + meta[:, R_RANK1].astype(jnp.int32)
    pos2 = seg_start[e2] + meta[:, R_RANK2].astype(jnp.int32)
    n_used = ends[-1:]
    tile_ids = jnp.minimum(jnp.arange(max_tiles, dtype=jnp.int32), n_used[0] - 1)
    tile_expert = jnp.sum(tile_ids[:, None] >= ends[None, :], axis=1).astype(jnp.int32)

    xs = _dispatch(pos1, pos2, h_p, h_s, max_tiles * tm)
    ys = _moe_experts(tile_expert, n_used.astype(jnp.int32), xs, wg, wu, wd)
    return _combine(pos1, pos2, ys, meta)


def _final_kernel(x_ref, f_ref, gt_ref, g_ref, o_ref):
    x = x_ref[...] + gt_ref[...] * f_ref[...]
    o_ref[...] = _rmsnorm(x, g_ref[...])


def _final(grp, layer, x, f, g_final):
    tm = ROW_TILE
    off = grp.row_block_offset
    return pl.pallas_call(
        _final_kernel,
        grid=(grp.rows // tm,),
        in_specs=[pl.BlockSpec((tm, D_MODEL), lambda i: (i, 0)),
                  pl.BlockSpec((tm, D_MODEL), lambda i: (i + off, 0)),
                  grp.mod_spec(layer, GATE_F, tm, 1),
                  pl.BlockSpec((1, D_MODEL), lambda i: (0, 0))],
        out_specs=pl.BlockSpec((tm, D_MODEL), lambda i: (i, 0)),
        out_shape=jax.ShapeDtypeStruct((grp.rows, D_MODEL), F32),
        compiler_params=_params(("parallel",)),
        name="final_norm",
    )(x, f, grp.mod, g_final)


def kernel(x_prompt, x_sample, c_prompt, c_sample, state_conv, state_ssm_re, state_ssm_im, w_ada, b_ada, g_norm_mix, g_norm_ffn, w_in, w_dw, b_dw, g_ln_conv, b_ln_conv, lam_re, lam_im, log_dt, b_ssm_re, b_ssm_im, c_ssm_re, c_ssm_im, d_ssm, w_glu, b_glu, w_out, w_gate_dense, w_up_dense, w_down_dense, w_router, b_router, w_gate_exp, w_up_exp, w_down_exp, g_final):
    depth = w_in.shape[0]
    bp, seq, _ = x_prompt.shape
    bs, steps, _ = x_sample.shape
    rows_p, rows_s = bp * seq, bs * steps
    assert rows_p % ROW_TILE == 0 and rows_s == ROW_TILE and seq % ROW_TILE == 0

    n_c = bp + bs
    pad = (-n_c) % 8
    c_all = jnp.concatenate([c_prompt, c_sample, jnp.zeros((pad, D_MODEL), F32)], axis=0)
    mod = _ada(c_all, w_ada, b_ada)
    mod_p = mod[:, :bp].reshape(depth, bp, 1, 6 * D_MODEL)
    mod_s = jnp.tile(mod[:, bp:n_c], (1, steps, 1))
    grp_p = _Group(rows_p, seq, mod_p, 0)
    grp_s = _Group(rows_s, None, mod_s, rows_p // ROW_TILE)

    ar, ai, wb, wc = _s5_params(lam_re, lam_im, log_dt, b_ssm_re, b_ssm_im, c_ssm_re, c_ssm_im)

    w_in_b = _cast_bf16(w_in.reshape(depth * D_MODEL, -1), 512).reshape(depth, D_MODEL, -1)
    w_glu_b = _cast_bf16(w_glu.reshape(depth * D_SSM, D_SSM), 512).reshape(depth, D_SSM, D_SSM)
    w_out_b = _cast_bf16(w_out.reshape(depth * D_MODEL, D_MODEL), 512).reshape(depth, D_MODEL, D_MODEL)
    n_dense = w_gate_dense.shape[0]
    d_ff = w_gate_dense.shape[2]
    wg_d = _cast_bf16(w_gate_dense.reshape(n_dense * D_MODEL, d_ff), 256).reshape(n_dense, D_MODEL, d_ff)
    wu_d = _cast_bf16(w_up_dense.reshape(n_dense * D_MODEL, d_ff), 256).reshape(n_dense, D_MODEL, d_ff)
    wd_d = _cast_bf16(w_down_dense.reshape(n_dense * d_ff, D_MODEL), 512).reshape(n_dense, d_ff, D_MODEL)

    x_p = x_prompt.reshape(rows_p, D_MODEL)
    x_s = jnp.transpose(x_sample, (1, 0, 2)).reshape(rows_s, D_MODEL)
    conv_tm = jnp.transpose(state_conv, (0, 2, 1, 3))
    row1 = lambda a: a.reshape(1, -1)

    f = None
    new_conv_p, new_re_p, new_im_p, new_conv_s, new_re_s, new_im_s = [], [], [], [], [], []
    for l in range(depth):
        g_mix, g_ffn = row1(g_norm_mix[l]), row1(g_norm_ffn[l])
        conv_w = (w_dw[l], row1(b_dw[l]), row1(g_ln_conv[l]), row1(b_ln_conv[l]))
        s5_w = (ar[l], ai[l], wb[l], wc[l], row1(d_ssm[l]), w_glu_b[l], row1(b_glu[l]))

        x_p, v_p, u_p = _layer_in(grp_p, l, x_p, f, g_mix, w_in_b[l])
        x_s, v_s, u_s = _layer_in(grp_s, l, x_s, f, g_mix, w_in_b[l])

        yc_p, cst_p = _conv_prompt(v_p, *conv_w, bp, seq)
        yc_s, cst_s = _conv_sample(v_s.reshape(steps, bs, D_CONV), conv_tm[l], *conv_w)
        ys_p, sre_p, sim_p = _s5_prompt(u_p, *s5_w, bp, seq)
        ys_s, sre_s, sim_s = _s5_sample(u_s.reshape(steps, bs, D_SSM),
                                        state_ssm_re[l].reshape(bs, N_STATE),
                                        state_ssm_im[l].reshape(bs, N_STATE), *s5_w)
        new_conv_p.append(cst_p)
        new_re_p.append(sre_p.reshape(bp, N_SSM_GROUPS, SSM_STATE))
        new_im_p.append(sim_p.reshape(bp, N_SSM_GROUPS, SSM_STATE))
        new_conv_s.append(jnp.transpose(cst_s, (1, 0, 2)))
        new_re_s.append(sre_s.reshape(bs, N_SSM_GROUPS, SSM_STATE))
        new_im_s.append(sim_s.reshape(bs, N_SSM_GROUPS, SSM_STATE))

        yc_s = yc_s.reshape(rows_s, D_CONV)
        ys_s = ys_s.reshape(rows_s, D_SSM)
        i = l // 2
        if l % 2 == 0:
            x_p, h_p = _layer_out(grp_p, l, yc_p, ys_p, w_out_b[l], x_p, g_ffn)
            x_s, h_s = _layer_out(grp_s, l, yc_s, ys_s, w_out_b[l], x_s, g_ffn)
            f = _ffn_dense(h_p, h_s, wg_d[i], wu_d[i], wd_d[i])
        else:
            router = (jnp.pad(w_router[i], ((0, 0), (0, LANES - N_EXPERTS))),
                      jnp.pad(b_router[i], (0, LANES - N_EXPERTS)).reshape(1, LANES))
            x_p, h_p, lg_p = _layer_out(grp_p, l, yc_p, ys_p, w_out_b[l], x_p, g_ffn, router)
            x_s, h_s, lg_s = _layer_out(grp_s, l, yc_s, ys_s, w_out_b[l], x_s, g_ffn, router)
            f = _moe(h_p, h_s, lg_p, lg_s, w_gate_exp[i], w_up_exp[i], w_down_exp[i])

    g_fin = row1(g_final)
    y_p = _final(grp_p, depth - 1, x_p, f, g_fin).reshape(bp, seq, D_MODEL)
    y_s = _final(grp_s, depth - 1, x_s, f, g_fin).reshape(steps, bs, D_MODEL)
    y_s = jnp.transpose(y_s, (1, 0, 2))
    return (y_p, y_s, jnp.stack(new_conv_p), jnp.stack(new_re_p), jnp.stack(new_im_p),
            jnp.stack(new_conv_s), jnp.stack(new_re_s), jnp.stack(new_im_s))
```

```python
import functools

import jax
import jax.numpy as jnp
from jax import lax
from jax.experimental import pallas as pl
from jax.experimental.pallas import tpu as pltpu

F32 = jnp.float32
BF16 = jnp.bfloat16

D_MODEL = 2048
D_CONV = 1024
D_SSM = 1024
CONV_WIDTH = 31
CONV_BUF = CONV_WIDTH - 1
SSM_GROUP = 16
N_SSM_GROUPS = 64
SSM_STATE = 64
LOG2_SSM_GROUP = 4
LOG2_SSM_STATE = 6
N_STATE = N_SSM_GROUPS * SSM_STATE
N_EXPERTS = 8
EPS = 1e-6

S5_BLOCKS = 8
S5_BLOCK_IN = D_SSM // S5_BLOCKS
S5_BLOCK_STATE = N_STATE // S5_BLOCKS
S5_LANE_CHUNKS = S5_BLOCK_STATE // 128
SUBLANES = 8
CONV_SUB = D_CONV // 128

LANES = 128
VMEM_LIMIT = 56 * 1024 * 1024

ROW_TILE = 512
MOE_ROW_TILE = 1024
MOE_SUB_ROWS = 256
MOE_FF_TILE = 256
FFN_ROW_TILE = 1024
FFN_FF_TILE = 512
CONV_TIME_TILE = 256
CONV_TIME_CHUNK = 16
S5_TIME_TILE = 256

SHIFT_M, SCALE_M, GATE_M, SHIFT_F, SCALE_F, GATE_F = range(6)


def _params(semantics, vmem=VMEM_LIMIT):
    return pltpu.CompilerParams(dimension_semantics=semantics, vmem_limit_bytes=vmem)


def _dot(a, b):
    return jnp.dot(a, b, preferred_element_type=F32)


def _rmsnorm(x, g):
    return x * lax.rsqrt(jnp.mean(x * x, axis=-1, keepdims=True) + EPS) * g


def _layernorm(x, g, b):
    xc = x - jnp.mean(x, axis=-1, keepdims=True)
    var = jnp.mean(xc * xc, axis=-1, keepdims=True)
    return xc * lax.rsqrt(var + EPS) * g + b


def _cast_kernel(x_ref, o_ref):
    o_ref[...] = x_ref[...].astype(o_ref.dtype)


def _cast_bf16(x, rows):
    r, c = x.shape
    return pl.pallas_call(
        _cast_kernel,
        grid=(r // rows,),
        in_specs=[pl.BlockSpec((rows, c), lambda i: (i, 0))],
        out_specs=pl.BlockSpec((rows, c), lambda i: (i, 0)),
        out_shape=jax.ShapeDtypeStruct((r, c), BF16),
        compiler_params=_params(("parallel",)),
        name="cast_bf16",
    )(x)


def _ada_kernel(c_ref, w_ref, b_ref, o_ref):
    cs = jax.nn.silu(c_ref[...]).astype(BF16)
    o_ref[...] = _dot(cs, w_ref[...].astype(BF16)) + b_ref[...]


def _ada(c_all, w_ada, b_ada):
    depth, _, n = w_ada.shape
    rows = c_all.shape[0]
    tn = 1024
    return pl.pallas_call(
        _ada_kernel,
        grid=(depth, n // tn),
        in_specs=[
            pl.BlockSpec((rows, D_MODEL), lambda l, j: (0, 0)),
            pl.BlockSpec((None, D_MODEL, tn), lambda l, j: (l, 0, j)),
            pl.BlockSpec((None, 1, tn), lambda l, j: (l, 0, j)),
        ],
        out_specs=pl.BlockSpec((None, rows, tn), lambda l, j: (l, 0, j)),
        out_shape=jax.ShapeDtypeStruct((depth, rows, n), F32),
        compiler_params=_params(("parallel", "parallel")),
        name="ada_mod",
    )(c_all, w_ada, b_ada.reshape(depth, 1, n))


class _Group:
    def __init__(self, rows, rows_per_batch, mod, row_block_offset, v_slabs):
        self.rows = rows
        self.rows_per_batch = rows_per_batch
        self.mod = mod
        self.row_block_offset = row_block_offset
        self.v_slabs = v_slabs

    def mod_spec(self, layer, chunk, tm):
        if self.rows_per_batch is None:
            def imap(*ids):
                return (layer, ids[0], chunk)
            return pl.BlockSpec((None, tm, D_MODEL), imap)
        per = self.rows_per_batch // tm

        def imap(*ids):
            return (layer, ids[0] // per, 0, chunk)
        return pl.BlockSpec((None, None, 1, D_MODEL), imap)


def _layer_in_kernel(*refs, has_prev, v_slabs):
    if has_prev:
        (x_ref, f_ref, gtf_ref, g_ref, sh_ref, sc_ref, w_ref, xo_ref, v_ref, u_ref) = refs
    else:
        (x_ref, g_ref, sh_ref, sc_ref, w_ref, v_ref, u_ref) = refs
    tm = x_ref.shape[0]
    x = x_ref[...]
    if has_prev:
        x = x + gtf_ref[...] * f_ref[...]
        xo_ref[...] = x
    h = (_rmsnorm(x, g_ref[...]) * (1 + sc_ref[...]) + sh_ref[...]).astype(BF16)
    tn = 512
    for j in range(D_CONV // tn):
        cols = slice(j * tn, (j + 1) * tn)
        a = _dot(h, w_ref[:, cols])
        g = _dot(h, w_ref[:, D_CONV + j * tn:D_CONV + (j + 1) * tn])
        v = a * jax.nn.sigmoid(g)
        if v_slabs:
            for c in range(tn // LANES):
                v_ref[pl.ds(j * (tn // LANES) + c, tm, stride=CONV_SUB), :] = v[:, c * LANES:(c + 1) * LANES]
        else:
            v_ref[:, cols] = v
        u_ref[:, cols] = _dot(h, w_ref[:, 2 * D_CONV + j * tn:2 * D_CONV + (j + 1) * tn])


def _layer_in(grp, layer, x, f_prev, g_norm, w_in_bf16):
    tm = 256
    has_prev = f_prev is not None
    row = lambda i: (i, 0)
    const = lambda i: (0, 0)
    in_specs = [pl.BlockSpec((tm, D_MODEL), row)]
    args = [x]
    if has_prev:
        f_arr, f_off = f_prev
        off = f_off * (ROW_TILE // tm)
        in_specs += [pl.BlockSpec((tm, D_MODEL), lambda i: (i + off, 0)),
                     grp.mod_spec(layer - 1, GATE_F, tm)]
        args += [f_arr, grp.mod]
    in_specs += [
        pl.BlockSpec((1, D_MODEL), const),
        grp.mod_spec(layer, SHIFT_M, tm),
        grp.mod_spec(layer, SCALE_M, tm),
        pl.BlockSpec(w_in_bf16.shape, const),
    ]
    args += [g_norm, grp.mod, grp.mod, w_in_bf16]
    if grp.v_slabs:
        v_spec = pl.BlockSpec((tm * CONV_SUB, LANES), row)
        v_shape = jax.ShapeDtypeStruct((grp.rows * CONV_SUB, LANES), F32)
    else:
        v_spec = pl.BlockSpec((tm, D_CONV), row)
        v_shape = jax.ShapeDtypeStruct((grp.rows, D_CONV), F32)
    out_specs = [v_spec, pl.BlockSpec((tm, D_SSM), row)]
    out_shape = [v_shape, jax.ShapeDtypeStruct((grp.rows, D_SSM), F32)]
    if has_prev:
        out_specs = [pl.BlockSpec((tm, D_MODEL), row)] + out_specs
        out_shape = [jax.ShapeDtypeStruct((grp.rows, D_MODEL), F32)] + out_shape
    outs = pl.pallas_call(
        functools.partial(_layer_in_kernel, has_prev=has_prev, v_slabs=grp.v_slabs),
        grid=(grp.rows // tm,),
        in_specs=in_specs,
        out_specs=out_specs,
        out_shape=out_shape,
        compiler_params=_params(("parallel",)),
        name="layer_in",
    )(*args)
    outs = list(outs) if has_prev else [x] + list(outs)
    if grp.v_slabs:
        outs[1] = outs[1].reshape(grp.rows, CONV_SUB, LANES)
    return outs


def _conv_epilogue(acc, b_ref, g_ref, be_ref):
    y = _layernorm(acc + b_ref[...], g_ref[...], be_ref[...])
    return jax.nn.silu(y).astype(BF16)


def _conv_prompt_kernel(v_ref, w_ref, b_ref, g_ref, be_ref, y_ref, st_ref, buf):
    tt = CONV_TIME_TILE
    tc = CONV_TIME_CHUNK
    t = pl.program_id(1)

    @pl.when(t == 0)
    def _():
        buf[0:32] = jnp.zeros((32, CONV_SUB, LANES), F32)

    @pl.when(t > 0)
    def _():
        buf[0:32] = buf[tt:tt + 32]

    buf[32:32 + tt] = v_ref[...]

    def chunk(ci, carry):
        t0 = ci * tc
        acc = None
        for k in range(CONV_WIDTH):
            term = w_ref[k] * buf[pl.ds(t0 + 2 + k, tc)]
            acc = term if acc is None else acc + term
        y = acc + b_ref[...]

        def mean_c(a):
            return jnp.sum(jnp.sum(a, axis=2, keepdims=True), axis=1, keepdims=True) * (1.0 / D_CONV)
        yc = y - mean_c(y)
        yn = yc * lax.rsqrt(mean_c(yc * yc) + EPS) * g_ref[...] + be_ref[...]
        y_ref[pl.ds(t0, tc)] = jax.nn.silu(yn)
        return carry

    lax.fori_loop(0, tt // tc, chunk, 0)
    st_ref[...] = buf[32 + tt - CONV_BUF:32 + tt]


def _conv_prompt(v, w_dw, b_dw, ln_g, ln_b, batch, seq):
    tt = CONV_TIME_TILE
    nt = seq // tt
    tile = (CONV_SUB, LANES)
    vec = pl.BlockSpec(tile, lambda b, t: (0, 0))
    y, st = pl.pallas_call(
        _conv_prompt_kernel,
        grid=(batch, nt),
        in_specs=[
            pl.BlockSpec((tt,) + tile, lambda b, t: (b * nt + t, 0, 0)),
            pl.BlockSpec((CONV_WIDTH,) + tile, lambda b, t: (0, 0, 0)),
            vec, vec, vec,
        ],
        out_specs=[
            pl.BlockSpec((tt,) + tile, lambda b, t: (b * nt + t, 0, 0)),
            pl.BlockSpec((None, CONV_BUF) + tile, lambda b, t: (b, 0, 0, 0)),
        ],
        out_shape=[jax.ShapeDtypeStruct((batch * seq,) + tile, F32),
                   jax.ShapeDtypeStruct((batch, CONV_BUF) + tile, F32)],
        scratch_shapes=[pltpu.VMEM((32 + tt,) + tile, F32)],
        compiler_params=_params(("parallel", "arbitrary")),
        name="conv_prompt",
    )(v, w_dw.reshape((CONV_WIDTH,) + tile), b_dw.reshape(tile), ln_g.reshape(tile), ln_b.reshape(tile))
    return y, st.reshape(batch, CONV_BUF, D_CONV)


def _conv_sample_kernel(v_ref, st_ref, w_ref, b_ref, g_ref, be_ref, y_ref, nst_ref):
    steps = v_ref.shape[0]
    for t in range(steps):
        acc = None
        for k in range(CONV_WIDTH):
            j = t + k
            src = st_ref[j] if j < CONV_BUF else v_ref[j - CONV_BUF]
            term = w_ref[k:k + 1, :] * src
            acc = term if acc is None else acc + term
        y_ref[t] = _conv_epilogue(acc, b_ref, g_ref, be_ref)
    for j in range(CONV_BUF - steps):
        nst_ref[j] = st_ref[j + steps]
    for j in range(steps):
        nst_ref[CONV_BUF - steps + j] = v_ref[j]


def _conv_sample(v_tm, state_tm, w_dw, b_dw, ln_g, ln_b):
    steps, batch, _ = v_tm.shape
    bb = 32
    vec = pl.BlockSpec((1, D_CONV), lambda i: (0, 0))
    return pl.pallas_call(
        _conv_sample_kernel,
        grid=(batch // bb,),
        in_specs=[
            pl.BlockSpec((steps, bb, D_CONV), lambda i: (0, i, 0)),
            pl.BlockSpec((CONV_BUF, bb, D_CONV), lambda i: (0, i, 0)),
            pl.BlockSpec((CONV_WIDTH, D_CONV), lambda i: (0, 0)),
            vec, vec, vec,
        ],
        out_specs=[
            pl.BlockSpec((steps, bb, D_CONV), lambda i: (0, i, 0)),
            pl.BlockSpec((CONV_BUF, bb, D_CONV), lambda i: (0, i, 0)),
        ],
        out_shape=[jax.ShapeDtypeStruct((steps, batch, D_CONV), BF16),
                   jax.ShapeDtypeStruct((CONV_BUF, batch, D_CONV), F32)],
        compiler_params=_params(("parallel",)),
        name="conv_sample",
    )(v_tm, state_tm, w_dw, b_dw, ln_g, ln_b)


def _s5_params_kernel(lr_ref, li_ref, ld_ref, br_ref, bi_ref, cr_ref, ci_ref,
                      ar_ref, ai_ref, wb_ref, wc_ref):
    lr = lr_ref[...]
    li = li_ref[...]
    dt = jnp.exp(ld_ref[...])
    mag = jnp.exp(lr * dt)
    ar = mag * jnp.cos(li * dt)
    ai = mag * jnp.sin(li * dt)
    den = lr * lr + li * li
    cr = ((ar - 1) * lr + ai * li) / den
    ci = (ai * lr - (ar - 1) * li) / den
    ar_ref[...] = ar
    ai_ref[...] = ai

    br = br_ref[...]
    bi = bi_ref[...]
    shape_b = (S5_BLOCK_IN, S5_BLOCK_STATE)
    same_b = (jnp.right_shift(lax.broadcasted_iota(jnp.int32, shape_b, 0), LOG2_SSM_GROUP)
              == jnp.right_shift(lax.broadcasted_iota(jnp.int32, shape_b, 1), LOG2_SSM_STATE))
    wb_ref[:, 0:S5_BLOCK_STATE] = jnp.where(same_b, cr * br - ci * bi, 0.0).astype(BF16)
    wb_ref[:, S5_BLOCK_STATE:] = jnp.where(same_b, cr * bi + ci * br, 0.0).astype(BF16)

    shape_c = (S5_BLOCK_STATE, S5_BLOCK_IN)
    same_c = (jnp.right_shift(lax.broadcasted_iota(jnp.int32, shape_c, 0), LOG2_SSM_STATE)
              == jnp.right_shift(lax.broadcasted_iota(jnp.int32, shape_c, 1), LOG2_SSM_GROUP))
    wc_ref[0:S5_BLOCK_STATE, :] = jnp.where(same_c, cr_ref[...], 0.0).astype(BF16)
    wc_ref[S5_BLOCK_STATE:, :] = jnp.where(same_c, -ci_ref[...], 0.0).astype(BF16)


def _s5_params(lam_re, lam_im, log_dt, b_re, b_im, c_re, c_im):
    depth = lam_re.shape[0]
    nb, gpb = S5_BLOCKS, N_SSM_GROUPS // S5_BLOCKS

    def per_state(a):
        return a.reshape(depth, nb, 1, S5_BLOCK_STATE)

    ld = jnp.broadcast_to(log_dt[:, :, None], lam_re.shape)

    def b_layout(b):
        bt = jnp.transpose(b, (0, 1, 3, 2)).reshape(depth, nb, S5_BLOCK_IN, SSM_STATE)
        return jnp.tile(bt, (1, 1, 1, gpb))

    def c_layout(c):
        ct = c.reshape(depth, nb, gpb, SSM_GROUP, SSM_STATE)
        ct = jnp.transpose(ct, (0, 1, 4, 2, 3)).reshape(depth, nb, SSM_STATE, S5_BLOCK_IN)
        return jnp.tile(ct, (1, 1, gpb, 1))

    st = pl.BlockSpec((None, None, 1, S5_BLOCK_STATE), lambda l, r: (l, r, 0, 0))
    bs = pl.BlockSpec((None, None, S5_BLOCK_IN, S5_BLOCK_STATE), lambda l, r: (l, r, 0, 0))
    cs = pl.BlockSpec((None, None, S5_BLOCK_STATE, S5_BLOCK_IN), lambda l, r: (l, r, 0, 0))
    ar, ai, wb, wc = pl.pallas_call(
        _s5_params_kernel,
        grid=(depth, nb),
        in_specs=[st, st, st, bs, bs, cs, cs],
        out_specs=[
            st, st,
            pl.BlockSpec((None, None, S5_BLOCK_IN, 2 * S5_BLOCK_STATE), lambda l, r: (l, r, 0, 0)),
            pl.BlockSpec((None, None, 2 * S5_BLOCK_STATE, S5_BLOCK_IN), lambda l, r: (l, r, 0, 0)),
        ],
        out_shape=[
            jax.ShapeDtypeStruct((depth, nb, 1, S5_BLOCK_STATE), F32),
            jax.ShapeDtypeStruct((depth, nb, 1, S5_BLOCK_STATE), F32),
            jax.ShapeDtypeStruct((depth, nb, S5_BLOCK_IN, 2 * S5_BLOCK_STATE), BF16),
            jax.ShapeDtypeStruct((depth, nb, 2 * S5_BLOCK_STATE, S5_BLOCK_IN), BF16),
        ],
        compiler_params=_params(("parallel", "parallel")),
        name="s5_params",
    )(per_state(lam_re), per_state(lam_im), per_state(ld),
      b_layout(b_re), b_layout(b_im), c_layout(c_re), c_layout(c_im))
    return ar.reshape(depth, 1, N_STATE), ai.reshape(depth, 1, N_STATE), wb, wc


def _s5_output(y, u, d_ref, wglu_ref, bglu_ref):
    y = jax.nn.gelu(y + d_ref[...] * u)
    z = _dot(y.astype(BF16), wglu_ref[...]) + bglu_ref[...]
    return (y * jax.nn.sigmoid(z)).astype(BF16)


def _s5_prompt_kernel(u_ref, ar_ref, ai_ref, wb_ref, wc_ref, d_ref, wglu_ref, bglu_ref,
                      y_ref, sre_ref, sim_ref, xre, xim, cre, cim):
    ts = S5_TIME_TILE
    nc = S5_LANE_CHUNKS

    @pl.when(pl.program_id(1) == 0)
    def _():
        cre[...] = jnp.zeros_like(cre)
        cim[...] = jnp.zeros_like(cim)

    u = u_ref[...]
    ub = u.astype(BF16)
    for r in range(S5_BLOCKS):
        x = _dot(ub[:, r * S5_BLOCK_IN:(r + 1) * S5_BLOCK_IN], wb_ref[r])
        for c in range(nc):
            xre[c, pl.ds(r, ts, stride=S5_BLOCKS), :] = x[:, c * LANES:(c + 1) * LANES]
            xim[c, pl.ds(r, ts, stride=S5_BLOCKS), :] = x[:, S5_BLOCK_STATE + c * LANES:
                                                          S5_BLOCK_STATE + (c + 1) * LANES]

    a_r = [ar_ref[c] for c in range(nc)]
    a_i = [ai_ref[c] for c in range(nc)]

    def step(t, carry):
        srs, sis = carry
        row = pl.multiple_of(t * S5_BLOCKS, S5_BLOCKS)
        new_r, new_i = [], []
        for c in range(nc):
            rows = pl.ds(row, S5_BLOCKS)
            nr = a_r[c] * srs[c] - a_i[c] * sis[c] + xre[c, rows, :]
            ni = a_r[c] * sis[c] + a_i[c] * srs[c] + xim[c, rows, :]
            xre[c, rows, :] = nr
            xim[c, rows, :] = ni
            new_r.append(nr)
            new_i.append(ni)
        return tuple(new_r), tuple(new_i)

    carry0 = (tuple(cre[c] for c in range(nc)), tuple(cim[c] for c in range(nc)))
    srs, sis = lax.fori_loop(0, ts, step, carry0, unroll=4)
    for c in range(nc):
        cre[c] = srs[c]
        cim[c] = sis[c]

    cols = []
    for r in range(S5_BLOCKS):
        sr = jnp.concatenate([xre[c, pl.ds(r, ts, stride=S5_BLOCKS), :] for c in range(nc)], axis=1)
        si = jnp.concatenate([xim[c, pl.ds(r, ts, stride=S5_BLOCKS), :] for c in range(nc)], axis=1)
        cols.append(_dot(sr.astype(BF16), wc_ref[r, 0:S5_BLOCK_STATE, :])
                    + _dot(si.astype(BF16), wc_ref[r, S5_BLOCK_STATE:, :]))
    y_ref[...] = _s5_output(jnp.concatenate(cols, axis=1), u, d_ref, wglu_ref, bglu_ref)
    sre_ref[...] = cre[...]
    sim_ref[...] = cim[...]


def _to_slabs(a):
    lead = a.shape[:-1]
    a = a.reshape(*lead, S5_BLOCKS, S5_LANE_CHUNKS, LANES)
    return jnp.swapaxes(a, -3, -2)


def _from_slabs(a):
    lead = a.shape[:-3]
    return jnp.swapaxes(a, -3, -2).reshape(*lead, N_STATE)


def _s5_prompt(u, ar, ai, wb, wc, d, w_glu_bf16, b_glu, batch, seq):
    ts = S5_TIME_TILE
    nt = seq // ts
    slab = (S5_LANE_CHUNKS, S5_BLOCKS, LANES)
    full = lambda shape: pl.BlockSpec(shape, lambda b, t: (0,) * len(shape))
    y, sre, sim = pl.pallas_call(
        _s5_prompt_kernel,
        grid=(batch, nt),
        in_specs=[
            pl.BlockSpec((ts, D_SSM), lambda b, t: (b * nt + t, 0)),
            full(slab), full(slab),
            full(wb.shape), full(wc.shape),
            full((1, D_SSM)), full((D_SSM, D_SSM)), full((1, D_SSM)),
        ],
        out_specs=[
            pl.BlockSpec((ts, D_SSM), lambda b, t: (b * nt + t, 0)),
            pl.BlockSpec((None,) + slab, lambda b, t: (b, 0, 0, 0)),
            pl.BlockSpec((None,) + slab, lambda b, t: (b, 0, 0, 0)),
        ],
        out_shape=[jax.ShapeDtypeStruct((batch * seq, D_SSM), BF16),
                   jax.ShapeDtypeStruct((batch,) + slab, F32),
                   jax.ShapeDtypeStruct((batch,) + slab, F32)],
        scratch_shapes=[pltpu.VMEM((S5_LANE_CHUNKS, ts * S5_BLOCKS, LANES), F32),
                        pltpu.VMEM((S5_LANE_CHUNKS, ts * S5_BLOCKS, LANES), F32),
                        pltpu.VMEM(slab, F32), pltpu.VMEM(slab, F32)],
        compiler_params=_params(("parallel", "arbitrary")),
        name="s5_prompt",
    )(u, _to_slabs(ar[0]), _to_slabs(ai[0]), wb, wc, d, w_glu_bf16, b_glu)
    return y, _from_slabs(sre), _from_slabs(sim)


def _s5_sample_kernel(u_ref, s0r_ref, s0i_ref, ar_ref, ai_ref, wb_ref, wc_ref, d_ref,
                      wglu_ref, bglu_ref, y_ref, sre_ref, sim_ref, yscr):
    steps = u_ref.shape[0]
    for r in range(S5_BLOCKS):
        ls = slice(r * S5_BLOCK_STATE, (r + 1) * S5_BLOCK_STATE)
        cs = slice(r * S5_BLOCK_IN, (r + 1) * S5_BLOCK_IN)
        a_r = ar_ref[:, ls]
        a_i = ai_ref[:, ls]
        sr = s0r_ref[:, ls]
        si = s0i_ref[:, ls]
        for t in range(steps):
            x = _dot(u_ref[t, :, cs].astype(BF16), wb_ref[r])
            sr, si = (a_r * sr - a_i * si + x[:, :S5_BLOCK_STATE],
                      a_r * si + a_i * sr + x[:, S5_BLOCK_STATE:])
            yscr[t, :, cs] = (_dot(sr.astype(BF16), wc_ref[r, 0:S5_BLOCK_STATE, :])
                              + _dot(si.astype(BF16), wc_ref[r, S5_BLOCK_STATE:, :]))
        sre_ref[:, ls] = sr
        sim_ref[:, ls] = si
    for t in range(steps):
        y_ref[t] = _s5_output(yscr[t], u_ref[t], d_ref, wglu_ref, bglu_ref)


def _s5_sample(u_tm, s0r, s0i, ar, ai, wb, wc, d, w_glu_bf16, b_glu):
    steps, batch, _ = u_tm.shape
    return pl.pallas_call(
        _s5_sample_kernel,
        out_shape=[jax.ShapeDtypeStruct((steps, batch, D_SSM), BF16),
                   jax.ShapeDtypeStruct((batch, N_STATE), F32),
                   jax.ShapeDtypeStruct((batch, N_STATE), F32)],
        scratch_shapes=[pltpu.VMEM((steps, batch, D_SSM), F32)],
        compiler_params=pltpu.CompilerParams(vmem_limit_bytes=VMEM_LIMIT),
        name="s5_sample",
    )(u_tm, s0r, s0i, ar, ai, wb, wc, d, w_glu_bf16, b_glu)


def _layer_out_kernel(*refs, moe, yc_slabs):
    if moe:
        (yc_ref, ys_ref, wo_ref, x_ref, gt_ref, g_ref, sh_ref, sc_ref, wr_ref, br_ref,
         xo_ref, h_ref, lg_ref) = refs
    else:
        (yc_ref, ys_ref, wo_ref, x_ref, gt_ref, g_ref, sh_ref, sc_ref, xo_ref, h_ref) = refs
    tm = x_ref.shape[0]
    if yc_slabs:
        yc = jnp.concatenate([yc_ref[pl.ds(c, tm, stride=CONV_SUB), :] for c in range(CONV_SUB)],
                             axis=1).astype(BF16)
    else:
        yc = yc_ref[...]
    o = _dot(yc, wo_ref[0:D_CONV, :]) + _dot(ys_ref[...], wo_ref[D_CONV:, :])
    x = x_ref[...] + gt_ref[...] * o
    xo_ref[...] = x
    h = _rmsnorm(x, g_ref[...]) * (1 + sc_ref[...]) + sh_ref[...]
    if moe:
        h_ref[...] = h
        h_hi = h.astype(BF16)
        h_lo = (h - h_hi.astype(F32)).astype(BF16)
        wr = wr_ref[...]
        w_hi = wr.astype(BF16)
        w_lo = (wr - w_hi.astype(F32)).astype(BF16)
        lg_ref[...] = (_dot(h_hi, w_hi) + (_dot(h_lo, w_hi) + _dot(h_hi, w_lo))) + br_ref[...]
    else:
        h_ref[...] = h.astype(BF16)


def _layer_out(grp, layer, yc, ys, w_out_bf16, x, g_norm, router=None):
    tm = 256
    moe = router is not None
    row = lambda i: (i, 0)
    const = lambda i: (0, 0)
    if grp.v_slabs:
        yc = yc.reshape(grp.rows * CONV_SUB, LANES)
        yc_spec = pl.BlockSpec((tm * CONV_SUB, LANES), row)
    else:
        yc_spec = pl.BlockSpec((tm, D_CONV), row)
    in_specs = [
        yc_spec, pl.BlockSpec((tm, D_SSM), row),
        pl.BlockSpec((D_MODEL, D_MODEL), const),
        pl.BlockSpec((tm, D_MODEL), row),
        grp.mod_spec(layer, GATE_M, tm),
        pl.BlockSpec((1, D_MODEL), const),
        grp.mod_spec(layer, SHIFT_F, tm),
        grp.mod_spec(layer, SCALE_F, tm),
    ]
    args = [yc, ys, w_out_bf16, x, grp.mod, g_norm, grp.mod, grp.mod]
    out_specs = [pl.BlockSpec((tm, D_MODEL), row), pl.BlockSpec((tm, D_MODEL), row)]
    out_shape = [jax.ShapeDtypeStruct((grp.rows, D_MODEL), F32),
                 jax.ShapeDtypeStruct((grp.rows, D_MODEL), F32 if moe else BF16)]
    if moe:
        in_specs += [pl.BlockSpec((D_MODEL, LANES), const), pl.BlockSpec((1, LANES), const)]
        args += list(router)
        out_specs.append(pl.BlockSpec((tm, LANES), row))
        out_shape.append(jax.ShapeDtypeStruct((grp.rows, LANES), F32))
    return pl.pallas_call(
        functools.partial(_layer_out_kernel, moe=moe, yc_slabs=grp.v_slabs),
        grid=(grp.rows // tm,),
        in_specs=in_specs,
        out_specs=out_specs,
        out_shape=out_shape,
        compiler_params=_params(("parallel",)),
        name="layer_out",
    )(*args)


def _ffn_kernel(h_ref, wg_ref, wu_ref, wd_ref, o_ref):
    j = pl.program_id(1)
    h = h_ref[...]
    act = (jax.nn.silu(_dot(h, wg_ref[...])) * _dot(h, wu_ref[...])).astype(BF16)
    contrib = _dot(act, wd_ref[...])

    @pl.when(j == 0)
    def _():
        o_ref[...] = contrib

    @pl.when(j > 0)
    def _():
        o_ref[...] += contrib


def _ffn_dense(h, wg, wu, wd):
    rows = h.shape[0]
    tm = min(rows, FFN_ROW_TILE)
    tf = FFN_FF_TILE
    d_ff = wg.shape[1]
    return pl.pallas_call(
        _ffn_kernel,
        grid=(rows // tm, d_ff // tf),
        in_specs=[pl.BlockSpec((tm, D_MODEL), lambda i, j: (i, 0)),
                  pl.BlockSpec((D_MODEL, tf), lambda i, j: (0, j)),
                  pl.BlockSpec((D_MODEL, tf), lambda i, j: (0, j)),
                  pl.BlockSpec((tf, D_MODEL), lambda i, j: (j, 0))],
        out_specs=pl.BlockSpec((tm, D_MODEL), lambda i, j: (i, 0)),
        out_shape=jax.ShapeDtypeStruct((rows, D_MODEL), F32),
        compiler_params=_params(("parallel", "arbitrary")),
        name="ffn_dense",
    )(h, wg, wu, wd)


R_E1, R_E2, R_P1, R_P2, R_RANK1, R_RANK2 = range(6)


def _route_kernel(lp_ref, ls_ref, meta_ref, cnt_ref, carry, *, n_first):
    i = pl.program_id(0)
    tm = meta_ref.shape[0]

    @pl.when(i == 0)
    def _():
        carry[...] = jnp.zeros_like(carry)

    lane = lax.broadcasted_iota(jnp.int32, (tm, LANES), 1).astype(F32)
    logits = jnp.where(i < n_first, lp_ref[...], ls_ref[...])
    logits = jnp.where(lane < N_EXPERTS, logits, -jnp.inf)
    m1 = jnp.max(logits, axis=-1, keepdims=True)
    e1 = jnp.min(jnp.where(logits == m1, lane, float(LANES)), axis=-1, keepdims=True)
    rest = jnp.where(lane == e1, -jnp.inf, logits)
    m2 = jnp.max(rest, axis=-1, keepdims=True)
    e2 = jnp.min(jnp.where(rest == m2, lane, float(LANES)), axis=-1, keepdims=True)
    x2 = jnp.exp(m2 - m1)
    den = 1.0 + x2
    p1 = 1.0 / den
    p2 = x2 / den

    hot1 = lane == e1
    hot2 = lane == e2
    hot = jnp.logical_or(hot1, hot2).astype(F32)
    rows = lax.broadcasted_iota(jnp.int32, (tm, tm), 0)
    cols = lax.broadcasted_iota(jnp.int32, (tm, tm), 1)
    earlier = (cols < rows).astype(BF16)
    rank = _dot(earlier, hot.astype(BF16)) + carry[0:1, :]
    rank1 = jnp.sum(jnp.where(hot1, rank, 0.0), axis=-1, keepdims=True)
    rank2 = jnp.sum(jnp.where(hot2, rank, 0.0), axis=-1, keepdims=True)
    total = carry[0:1, :] + jnp.sum(hot, axis=0, keepdims=True)
    carry[...] = jnp.broadcast_to(total, carry.shape)
    cnt_ref[...] = jnp.broadcast_to(total, cnt_ref.shape)

    meta = jnp.where(lane == R_E1, e1.astype(F32), 0.0)
    meta = jnp.where(lane == R_E2, e2.astype(F32), meta)
    meta = jnp.where(lane == R_P1, p1, meta)
    meta = jnp.where(lane == R_P2, p2, meta)
    meta = jnp.where(lane == R_RANK1, rank1, meta)
    meta = jnp.where(lane == R_RANK2, rank2, meta)
    meta_ref[...] = meta


def _route(lg_p, lg_s):
    tm = ROW_TILE
    n_first = lg_p.shape[0] // tm
    n_tiles = n_first + lg_s.shape[0] // tm
    first = lambda i: (jnp.minimum(i, n_first - 1), 0)
    second = lambda i: (jnp.maximum(i - n_first, 0), 0)
    return pl.pallas_call(
        functools.partial(_route_kernel, n_first=n_first),
        grid=(n_tiles,),
        in_specs=[pl.BlockSpec((tm, LANES), first), pl.BlockSpec((tm, LANES), second)],
        out_specs=[pl.BlockSpec((tm, LANES), lambda i: (i, 0)),
                   pl.BlockSpec((8, LANES), lambda i: (0, 0))],
        out_shape=[jax.ShapeDtypeStruct((n_tiles * tm, LANES), F32),
                   jax.ShapeDtypeStruct((8, LANES), F32)],
        scratch_shapes=[pltpu.VMEM((8, LANES), F32)],
        compiler_params=_params(("arbitrary",)),
        name="moe_route",
    )(lg_p, lg_s)


def _row_copies(n, start_fn):
    def issue(r, c):
        for cp in start_fn(r):
            cp.start()
        return c
    lax.fori_loop(0, n, issue, 0)

    def drain(r, c):
        for cp in start_fn(0):
            cp.wait()
        return c
    lax.fori_loop(0, n, drain, 0)


def _dispatch_kernel(p1_ref, p2_ref, hp_ref, hs_ref, xs_in_ref, xs_ref, sem, *, n_first):
    del xs_in_ref
    i = pl.program_id(0)
    tm = hp_ref.shape[0]
    base = i * tm

    def run(src_ref):
        def copies(r):
            row = src_ref.at[pl.ds(r, 1), :]
            return (pltpu.make_async_copy(row, xs_ref.at[pl.ds(p1_ref[base + r], 1), :], sem.at[0]),
                    pltpu.make_async_copy(row, xs_ref.at[pl.ds(p2_ref[base + r], 1), :], sem.at[1]))
        _row_copies(tm, copies)

    @pl.when(i < n_first)
    def _():
        run(hp_ref)

    @pl.when(i >= n_first)
    def _():
        run(hs_ref)


def _dispatch(pos1, pos2, h_p, h_s, n_slots):
    tm = ROW_TILE
    n_first = h_p.shape[0] // tm
    n_tiles = n_first + h_s.shape[0] // tm
    first = lambda i, p1, p2: (jnp.minimum(i, n_first - 1), 0)
    second = lambda i, p1, p2: (jnp.maximum(i - n_first, 0), 0)
    xs0 = jnp.zeros((n_slots, D_MODEL), F32)
    return pl.pallas_call(
        functools.partial(_dispatch_kernel, n_first=n_first),
        grid_spec=pltpu.PrefetchScalarGridSpec(
            num_scalar_prefetch=2,
            grid=(n_tiles,),
            in_specs=[pl.BlockSpec((tm, D_MODEL), first), pl.BlockSpec((tm, D_MODEL), second),
                      pl.BlockSpec(memory_space=pl.ANY)],
            out_specs=pl.BlockSpec(memory_space=pl.ANY),
            scratch_shapes=[pltpu.SemaphoreType.DMA((2,))],
        ),
        out_shape=jax.ShapeDtypeStruct((n_slots, D_MODEL), F32),
        input_output_aliases={4: 0},
        compiler_params=_params(("arbitrary",)),
        name="moe_dispatch",
    )(pos1, pos2, h_p, h_s, xs0)


def _moe_kernel(te_ref, nv_ref, xs_ref, wg_ref, wu_ref, wd_ref, o_ref, h_scr, wg_scr, wu_scr, wd_scr):
    i = pl.program_id(0)
    j = pl.program_id(1)
    n_valid = nv_ref[i]

    @pl.when(j == 0)
    def _():
        o_ref[...] = jnp.zeros_like(o_ref)
        h_scr[...] = xs_ref[...].astype(BF16)

    @pl.when(n_valid > 0)
    def _():
        wg_scr[...] = wg_ref[...].astype(BF16)
        wu_scr[...] = wu_ref[...].astype(BF16)
        wd_scr[...] = wd_ref[...].astype(BF16)

    for sb in range(MOE_ROW_TILE // MOE_SUB_ROWS):
        @pl.when(sb * MOE_SUB_ROWS < n_valid)
        def _(sb=sb):
            rows = slice(sb * MOE_SUB_ROWS, (sb + 1) * MOE_SUB_ROWS)
            h = h_scr[rows, :]
            act = (jax.nn.silu(_dot(h, wg_scr[...])) * _dot(h, wu_scr[...])).astype(BF16)
            o_ref[rows, :] += _dot(act, wd_scr[...])


def _moe_experts(tile_expert, n_valid, xs, wg, wu, wd):
    tm, tf = MOE_ROW_TILE, MOE_FF_TILE
    n_tiles = xs.shape[0] // tm
    d_ff = wg.shape[2]
    nj = d_ff // tf

    def ff(i, j, te, nv):
        return jnp.where(nv[i] > 0, j, nj - 1)

    return pl.pallas_call(
        _moe_kernel,
        grid_spec=pltpu.PrefetchScalarGridSpec(
            num_scalar_prefetch=2,
            grid=(n_tiles, nj),
            in_specs=[
                pl.BlockSpec((tm, D_MODEL), lambda i, j, te, nu: (i, 0)),
                pl.BlockSpec((None, D_MODEL, tf), lambda i, j, te, nu: (te[i], 0, ff(i, j, te, nu))),
                pl.BlockSpec((None, D_MODEL, tf), lambda i, j, te, nu: (te[i], 0, ff(i, j, te, nu))),
                pl.BlockSpec((None, tf, D_MODEL), lambda i, j, te, nu: (te[i], ff(i, j, te, nu), 0)),
            ],
            out_specs=pl.BlockSpec((tm, D_MODEL), lambda i, j, te, nu: (i, 0)),
            scratch_shapes=[pltpu.VMEM((tm, D_MODEL), BF16),
                            pltpu.VMEM((D_MODEL, tf), BF16), pltpu.VMEM((D_MODEL, tf), BF16),
                            pltpu.VMEM((tf, D_MODEL), BF16)],
        ),
        out_shape=jax.ShapeDtypeStruct(xs.shape, F32),
        compiler_params=_params(("arbitrary", "arbitrary")),
        name="moe_experts",
    )(tile_expert, n_valid, xs, wg, wu, wd)


def _combine_kernel(p1_ref, p2_ref, ys_ref, meta_ref, o_ref, y1, y2, sem):
    tm = o_ref.shape[0]
    base = pl.program_id(0) * tm

    def copies(r):
        return (pltpu.make_async_copy(ys_ref.at[pl.ds(p1_ref[base + r], 1), :],
                                      y1.at[pl.ds(r, 1), :], sem.at[0]),
                pltpu.make_async_copy(ys_ref.at[pl.ds(p2_ref[base + r], 1), :],
                                      y2.at[pl.ds(r, 1), :], sem.at[1]))
    _row_copies(tm, copies)
    meta = meta_ref[...]
    o_ref[...] = meta[:, R_P1:R_P1 + 1] * y1[...] + meta[:, R_P2:R_P2 + 1] * y2[...]


def _combine(pos1, pos2, ys, meta):
    tm = ROW_TILE
    n_rows = meta.shape[0]
    return pl.pallas_call(
        _combine_kernel,
        grid_spec=pltpu.PrefetchScalarGridSpec(
            num_scalar_prefetch=2,
            grid=(n_rows // tm,),
            in_specs=[pl.BlockSpec(memory_space=pl.ANY),
                      pl.BlockSpec((tm, LANES), lambda i, p1, p2: (i, 0))],
            out_specs=pl.BlockSpec((tm, D_MODEL), lambda i, p1, p2: (i, 0)),
            scratch_shapes=[pltpu.VMEM((tm, D_MODEL), F32), pltpu.VMEM((tm, D_MODEL), F32),
                            pltpu.SemaphoreType.DMA((2,))],
        ),
        out_shape=jax.ShapeDtypeStruct((n_rows, D_MODEL), F32),
        compiler_params=_params(("arbitrary",)),
        name="moe_combine",
    )(pos1, pos2, ys, meta)


def _moe(h_p, h_s, lg_p, lg_s, wg, wu, wd):
    tm = MOE_ROW_TILE
    n_rows = h_p.shape[0] + h_s.shape[0]
    max_tiles = (2 * n_rows) // tm + N_EXPERTS
    meta, cnt = _route(lg_p, lg_s)
    counts = cnt[0, :N_EXPERTS].astype(jnp.int32)
    tiles = (counts + tm - 1) // tm
    ends = jnp.cumsum(tiles)
    seg_start = (ends - tiles) * tm
    e1 = meta[:, R_E1].astype(jnp.int32)
    e2 = meta[:, R_E2].astype(jnp.int32)
    pos1 = seg_start[e1] + meta[:, R_RANK1].astype(jnp.int32)
    pos2 = seg_start[e2] + meta[:, R_RANK2].astype(jnp.int32)
    tile_ids = jnp.arange(max_tiles, dtype=jnp.int32)
    used = tile_ids < ends[-1]
    tile_expert = jnp.sum(jnp.minimum(tile_ids, ends[-1] - 1)[:, None] >= ends[None, :],
                          axis=1).astype(jnp.int32)
    seg_end = seg_start[tile_expert] + counts[tile_expert]
    n_valid = jnp.where(used, jnp.clip(seg_end - tile_ids * tm, 0, tm), 0).astype(jnp.int32)

    xs = _dispatch(pos1, pos2, h_p, h_s, max_tiles * tm)
    ys = _moe_experts(tile_expert, n_valid, xs, wg, wu, wd)
    return _combine(pos1, pos2, ys, meta)


def _final_kernel(x_ref, f_ref, gt_ref, g_ref, o_ref):
    x = x_ref[...] + gt_ref[...] * f_ref[...]
    o_ref[...] = _rmsnorm(x, g_ref[...])


def _final(grp, layer, x, f_prev, g_final):
    tm = ROW_TILE
    f, off = f_prev
    return pl.pallas_call(
        _final_kernel,
        grid=(grp.rows // tm,),
        in_specs=[pl.BlockSpec((tm, D_MODEL), lambda i: (i, 0)),
                  pl.BlockSpec((tm, D_MODEL), lambda i: (i + off, 0)),
                  grp.mod_spec(layer, GATE_F, tm),
                  pl.BlockSpec((1, D_MODEL), lambda i: (0, 0))],
        out_specs=pl.BlockSpec((tm, D_MODEL), lambda i: (i, 0)),
        out_shape=jax.ShapeDtypeStruct((grp.rows, D_MODEL), F32),
        compiler_params=_params(("parallel",)),
        name="final_norm",
    )(x, f, grp.mod, g_final)


def kernel(x_prompt, x_sample, c_prompt, c_sample, state_conv, state_ssm_re, state_ssm_im, w_ada, b_ada, g_norm_mix, g_norm_ffn, w_in, w_dw, b_dw, g_ln_conv, b_ln_conv, lam_re, lam_im, log_dt, b_ssm_re, b_ssm_im, c_ssm_re, c_ssm_im, d_ssm, w_glu, b_glu, w_out, w_gate_dense, w_up_dense, w_down_dense, w_router, b_router, w_gate_exp, w_up_exp, w_down_exp, g_final):
    depth = w_in.shape[0]
    bp, seq, _ = x_prompt.shape
    bs, steps, _ = x_sample.shape
    rows_p, rows_s = bp * seq, bs * steps
    assert rows_p % ROW_TILE == 0 and rows_s == ROW_TILE and seq % ROW_TILE == 0

    n_c = bp + bs
    pad = (-n_c) % 8
    c_all = jnp.concatenate([c_prompt, c_sample, jnp.zeros((pad, D_MODEL), F32)], axis=0)
    mod = _ada(c_all, w_ada, b_ada)
    mod_p = mod[:, :bp].reshape(depth, bp, 1, 6 * D_MODEL)
    mod_s = jnp.tile(mod[:, bp:n_c], (1, steps, 1))
    grp_p = _Group(rows_p, seq, mod_p, 0, True)
    grp_s = _Group(rows_s, None, mod_s, rows_p // ROW_TILE, False)

    ar, ai, wb, wc = _s5_params(lam_re, lam_im, log_dt, b_ssm_re, b_ssm_im, c_ssm_re, c_ssm_im)

    w_in_b = _cast_bf16(w_in.reshape(depth * D_MODEL, -1), 512).reshape(depth, D_MODEL, -1)
    w_glu_b = _cast_bf16(w_glu.reshape(depth * D_SSM, D_SSM), 512).reshape(depth, D_SSM, D_SSM)
    w_out_b = _cast_bf16(w_out.reshape(depth * D_MODEL, D_MODEL), 512).reshape(depth, D_MODEL, D_MODEL)
    n_dense = w_gate_dense.shape[0]
    d_ff = w_gate_dense.shape[2]
    wg_d = _cast_bf16(w_gate_dense.reshape(n_dense * D_MODEL, d_ff), 256).reshape(n_dense, D_MODEL, d_ff)
    wu_d = _cast_bf16(w_up_dense.reshape(n_dense * D_MODEL, d_ff), 256).reshape(n_dense, D_MODEL, d_ff)
    wd_d = _cast_bf16(w_down_dense.reshape(n_dense * d_ff, D_MODEL), 512).reshape(n_dense, d_ff, D_MODEL)

    x_p = x_prompt.reshape(rows_p, D_MODEL)
    x_s = jnp.transpose(x_sample, (1, 0, 2)).reshape(rows_s, D_MODEL)
    conv_tm = jnp.transpose(state_conv, (0, 2, 1, 3))
    row1 = lambda a: a.reshape(1, -1)

    f_p = f_s = None
    new_conv_p, new_re_p, new_im_p, new_conv_s, new_re_s, new_im_s = [], [], [], [], [], []
    for l in range(depth):
        g_mix, g_ffn = row1(g_norm_mix[l]), row1(g_norm_ffn[l])
        conv_w = (w_dw[l], row1(b_dw[l]), row1(g_ln_conv[l]), row1(b_ln_conv[l]))
        s5_w = (ar[l], ai[l], wb[l], wc[l], row1(d_ssm[l]), w_glu_b[l], row1(b_glu[l]))

        x_p, v_p, u_p = _layer_in(grp_p, l, x_p, f_p, g_mix, w_in_b[l])
        x_s, v_s, u_s = _layer_in(grp_s, l, x_s, f_s, g_mix, w_in_b[l])

        yc_p, cst_p = _conv_prompt(v_p, *conv_w, bp, seq)
        yc_s, cst_s = _conv_sample(v_s.reshape(steps, bs, D_CONV), conv_tm[l], *conv_w)
        ys_p, sre_p, sim_p = _s5_prompt(u_p, *s5_w, bp, seq)
        ys_s, sre_s, sim_s = _s5_sample(u_s.reshape(steps, bs, D_SSM),
                                        state_ssm_re[l].reshape(bs, N_STATE),
                                        state_ssm_im[l].reshape(bs, N_STATE), *s5_w)
        new_conv_p.append(cst_p)
        new_re_p.append(sre_p.reshape(bp, N_SSM_GROUPS, SSM_STATE))
        new_im_p.append(sim_p.reshape(bp, N_SSM_GROUPS, SSM_STATE))
        new_conv_s.append(jnp.transpose(cst_s, (1, 0, 2)))
        new_re_s.append(sre_s.reshape(bs, N_SSM_GROUPS, SSM_STATE))
        new_im_s.append(sim_s.reshape(bs, N_SSM_GROUPS, SSM_STATE))

        yc_s = yc_s.reshape(rows_s, D_CONV)
        ys_s = ys_s.reshape(rows_s, D_SSM)
        i = l // 2
        if l % 2 == 0:
            x_p, h_p = _layer_out(grp_p, l, yc_p, ys_p, w_out_b[l], x_p, g_ffn)
            x_s, h_s = _layer_out(grp_s, l, yc_s, ys_s, w_out_b[l], x_s, g_ffn)
            f_p = (_ffn_dense(h_p, wg_d[i], wu_d[i], wd_d[i]), 0)
            f_s = (_ffn_dense(h_s, wg_d[i], wu_d[i], wd_d[i]), 0)
        else:
            router = (jnp.pad(w_router[i], ((0, 0), (0, LANES - N_EXPERTS))),
                      jnp.pad(b_router[i], (0, LANES - N_EXPERTS)).reshape(1, LANES))
            x_p, h_p, lg_p = _layer_out(grp_p, l, yc_p, ys_p, w_out_b[l], x_p, g_ffn, router)
            x_s, h_s, lg_s = _layer_out(grp_s, l, yc_s, ys_s, w_out_b[l], x_s, g_ffn, router)
            f = _moe(h_p, h_s, lg_p, lg_s, w_gate_exp[i], w_up_exp[i], w_down_exp[i])
            f_p, f_s = (f, grp_p.row_block_offset), (f, grp_s.row_block_offset)

    g_fin = row1(g_final)
    y_p = _final(grp_p, depth - 1, x_p, f_p, g_fin).reshape(bp, seq, D_MODEL)
    y_s = _final(grp_s, depth - 1, x_s, f_s, g_fin).reshape(steps, bs, D_MODEL)
    y_s = jnp.transpose(y_s, (1, 0, 2))
    return (y_p, y_s, jnp.stack(new_conv_p), jnp.stack(new_re_p), jnp.stack(new_im_p),
            jnp.stack(new_conv_s), jnp.stack(new_re_s), jnp.stack(new_im_s))
```

```python
import functools

import jax
import jax.numpy as jnp
from jax import lax
from jax.experimental import pallas as pl
from jax.experimental.pallas import tpu as pltpu

F32 = jnp.float32
BF16 = jnp.bfloat16

D_MODEL = 2048
D_CONV = 1024
D_SSM = 1024
CONV_WIDTH = 31
CONV_BUF = CONV_WIDTH - 1
SSM_GROUP = 16
N_SSM_GROUPS = 64
SSM_STATE = 64
LOG2_SSM_GROUP = 4
LOG2_SSM_STATE = 6
N_STATE = N_SSM_GROUPS * SSM_STATE
N_EXPERTS = 8
EPS = 1e-6

S5_BLOCKS = 8
S5_BLOCK_IN = D_SSM // S5_BLOCKS
S5_BLOCK_STATE = N_STATE // S5_BLOCKS
S5_LANE_CHUNKS = S5_BLOCK_STATE // 128
SUBLANES = 8
CONV_SUB = D_CONV // 128

LANES = 128
VMEM_LIMIT = 56 * 1024 * 1024

ROW_TILE = 512
MOE_ROW_TILE = 1024
MOE_SUB_ROWS = 256
MOE_FF_TILE = 256
FFN_ROW_TILE = 1024
FFN_FF_TILE = 512
CONV_TIME_TILE = 256
CONV_TIME_CHUNK = 16
S5_TIME_TILE = 256

SHIFT_M, SCALE_M, GATE_M, SHIFT_F, SCALE_F, GATE_F = range(6)


def _params(semantics, vmem=VMEM_LIMIT):
    return pltpu.CompilerParams(dimension_semantics=semantics, vmem_limit_bytes=vmem)


def _dot(a, b):
    return jnp.dot(a, b, preferred_element_type=F32)


def _rmsnorm(x, g):
    return x * lax.rsqrt(jnp.mean(x * x, axis=-1, keepdims=True) + EPS) * g


def _layernorm(x, g, b):
    xc = x - jnp.mean(x, axis=-1, keepdims=True)
    var = jnp.mean(xc * xc, axis=-1, keepdims=True)
    return xc * lax.rsqrt(var + EPS) * g + b


def _cast_kernel(x_ref, o_ref):
    o_ref[...] = x_ref[...].astype(o_ref.dtype)


def _cast_bf16(x, layer, rows):
    _, r, c = x.shape
    return pl.pallas_call(
        _cast_kernel,
        grid=(r // rows,),
        in_specs=[pl.BlockSpec((None, rows, c), lambda i: (layer, i, 0))],
        out_specs=pl.BlockSpec((rows, c), lambda i: (i, 0)),
        out_shape=jax.ShapeDtypeStruct((r, c), BF16),
        compiler_params=_params(("parallel",)),
        name="cast_bf16",
    )(x)


def _ada_kernel(c_ref, w_ref, b_ref, o_ref):
    cs = jax.nn.silu(c_ref[...]).astype(BF16)
    o_ref[...] = _dot(cs, w_ref[...].astype(BF16)) + b_ref[...]


def _ada(c_all, w_ada, b_ada):
    depth, _, n = w_ada.shape
    rows = c_all.shape[0]
    tn = 1024
    return pl.pallas_call(
        _ada_kernel,
        grid=(depth, n // tn),
        in_specs=[
            pl.BlockSpec((rows, D_MODEL), lambda l, j: (0, 0)),
            pl.BlockSpec((None, D_MODEL, tn), lambda l, j: (l, 0, j)),
            pl.BlockSpec((None, 1, tn), lambda l, j: (l, 0, j)),
        ],
        out_specs=pl.BlockSpec((None, rows, tn), lambda l, j: (l, 0, j)),
        out_shape=jax.ShapeDtypeStruct((depth, rows, n), F32),
        compiler_params=_params(("parallel", "parallel")),
        name="ada_mod",
    )(c_all, w_ada, b_ada.reshape(depth, 1, n))


class _Group:
    def __init__(self, rows, rows_per_batch, mod, row_block_offset, v_slabs):
        self.rows = rows
        self.rows_per_batch = rows_per_batch
        self.mod = mod
        self.row_block_offset = row_block_offset
        self.v_slabs = v_slabs

    def mod_spec(self, layer, chunk, tm):
        if self.rows_per_batch is None:
            def imap(*ids):
                return (layer, ids[0], chunk)
            return pl.BlockSpec((None, tm, D_MODEL), imap)
        per = self.rows_per_batch // tm

        def imap(*ids):
            return (layer, ids[0] // per, 0, chunk)
        return pl.BlockSpec((None, None, 1, D_MODEL), imap)


def _layer_in_kernel(*refs, has_prev, v_slabs):
    if has_prev:
        (x_ref, f_ref, gtf_ref, g_ref, sh_ref, sc_ref, w_ref, xo_ref, v_ref, u_ref) = refs
    else:
        (x_ref, g_ref, sh_ref, sc_ref, w_ref, v_ref, u_ref) = refs
    tm = x_ref.shape[0]
    x = x_ref[...]
    if has_prev:
        x = x + gtf_ref[...] * f_ref[...]
        xo_ref[...] = x
    h = (_rmsnorm(x, g_ref[...]) * (1 + sc_ref[...]) + sh_ref[...]).astype(BF16)
    tn = 512
    for j in range(D_CONV // tn):
        cols = slice(j * tn, (j + 1) * tn)
        a = _dot(h, w_ref[:, cols])
        g = _dot(h, w_ref[:, D_CONV + j * tn:D_CONV + (j + 1) * tn])
        v = a * jax.nn.sigmoid(g)
        if v_slabs:
            for c in range(tn // LANES):
                v_ref[pl.ds(j * (tn // LANES) + c, tm, stride=CONV_SUB), :] = v[:, c * LANES:(c + 1) * LANES]
        else:
            v_ref[:, cols] = v
        u_ref[:, cols] = _dot(h, w_ref[:, 2 * D_CONV + j * tn:2 * D_CONV + (j + 1) * tn])


def _layer_in(grp, layer, x, f_prev, g_norm, w_in_bf16):
    tm = 256
    has_prev = f_prev is not None
    row = lambda i: (i, 0)
    const = lambda i: (0, 0)
    in_specs = [pl.BlockSpec((tm, D_MODEL), row)]
    args = [x]
    if has_prev:
        f_arr, f_off = f_prev
        off = f_off * (ROW_TILE // tm)
        in_specs += [pl.BlockSpec((tm, D_MODEL), lambda i: (i + off, 0)),
                     grp.mod_spec(layer - 1, GATE_F, tm)]
        args += [f_arr, grp.mod]
    in_specs += [
        pl.BlockSpec((1, D_MODEL), const),
        grp.mod_spec(layer, SHIFT_M, tm),
        grp.mod_spec(layer, SCALE_M, tm),
        pl.BlockSpec(w_in_bf16.shape, const),
    ]
    args += [g_norm, grp.mod, grp.mod, w_in_bf16]
    if grp.v_slabs:
        v_spec = pl.BlockSpec((tm * CONV_SUB, LANES), row)
        v_shape = jax.ShapeDtypeStruct((grp.rows * CONV_SUB, LANES), F32)
    else:
        v_spec = pl.BlockSpec((tm, D_CONV), row)
        v_shape = jax.ShapeDtypeStruct((grp.rows, D_CONV), F32)
    out_specs = [v_spec, pl.BlockSpec((tm, D_SSM), row)]
    out_shape = [v_shape, jax.ShapeDtypeStruct((grp.rows, D_SSM), F32)]
    if has_prev:
        out_specs = [pl.BlockSpec((tm, D_MODEL), row)] + out_specs
        out_shape = [jax.ShapeDtypeStruct((grp.rows, D_MODEL), F32)] + out_shape
    outs = pl.pallas_call(
        functools.partial(_layer_in_kernel, has_prev=has_prev, v_slabs=grp.v_slabs),
        grid=(grp.rows // tm,),
        in_specs=in_specs,
        out_specs=out_specs,
        out_shape=out_shape,
        compiler_params=_params(("parallel",)),
        name="layer_in",
    )(*args)
    outs = list(outs) if has_prev else [x] + list(outs)
    if grp.v_slabs:
        outs[1] = outs[1].reshape(grp.rows, CONV_SUB, LANES)
    return outs


def _conv_epilogue(acc, b_ref, g_ref, be_ref):
    y = _layernorm(acc + b_ref[...], g_ref[...], be_ref[...])
    return jax.nn.silu(y).astype(BF16)


def _conv_prompt_kernel(v_ref, w_ref, b_ref, g_ref, be_ref, y_ref, st_ref, buf):
    tt = CONV_TIME_TILE
    tc = CONV_TIME_CHUNK
    t = pl.program_id(1)

    @pl.when(t == 0)
    def _():
        buf[0:32] = jnp.zeros((32, CONV_SUB, LANES), F32)

    @pl.when(t > 0)
    def _():
        buf[0:32] = buf[tt:tt + 32]

    buf[32:32 + tt] = v_ref[...]

    def chunk(ci, carry):
        t0 = ci * tc
        acc = None
        for k in range(CONV_WIDTH):
            term = w_ref[k] * buf[pl.ds(t0 + 2 + k, tc)]
            acc = term if acc is None else acc + term
        y = acc + b_ref[...]

        def mean_c(a):
            return jnp.sum(jnp.sum(a, axis=2, keepdims=True), axis=1, keepdims=True) * (1.0 / D_CONV)
        yc = y - mean_c(y)
        yn = yc * lax.rsqrt(mean_c(yc * yc) + EPS) * g_ref[...] + be_ref[...]
        y_ref[pl.ds(t0, tc)] = jax.nn.silu(yn)
        return carry

    lax.fori_loop(0, tt // tc, chunk, 0, unroll=2)
    st_ref[...] = buf[32 + tt - CONV_BUF:32 + tt]


def _conv_prompt(v, w_dw, b_dw, ln_g, ln_b, batch, seq):
    tt = CONV_TIME_TILE
    nt = seq // tt
    tile = (CONV_SUB, LANES)
    vec = pl.BlockSpec(tile, lambda b, t: (0, 0))
    y, st = pl.pallas_call(
        _conv_prompt_kernel,
        grid=(batch, nt),
        in_specs=[
            pl.BlockSpec((tt,) + tile, lambda b, t: (b * nt + t, 0, 0)),
            pl.BlockSpec((CONV_WIDTH,) + tile, lambda b, t: (0, 0, 0)),
            vec, vec, vec,
        ],
        out_specs=[
            pl.BlockSpec((tt,) + tile, lambda b, t: (b * nt + t, 0, 0)),
            pl.BlockSpec((None, CONV_BUF) + tile, lambda b, t: (b, 0, 0, 0)),
        ],
        out_shape=[jax.ShapeDtypeStruct((batch * seq,) + tile, F32),
                   jax.ShapeDtypeStruct((batch, CONV_BUF) + tile, F32)],
        scratch_shapes=[pltpu.VMEM((32 + tt,) + tile, F32)],
        compiler_params=_params(("parallel", "arbitrary")),
        name="conv_prompt",
    )(v, w_dw.reshape((CONV_WIDTH,) + tile), b_dw.reshape(tile), ln_g.reshape(tile), ln_b.reshape(tile))
    return y, st.reshape(batch, CONV_BUF, D_CONV)


def _conv_sample_kernel(v_ref, st_ref, w_ref, b_ref, g_ref, be_ref, y_ref, nst_ref):
    steps = v_ref.shape[0]
    for t in range(steps):
        acc = None
        for k in range(CONV_WIDTH):
            j = t + k
            src = st_ref[j] if j < CONV_BUF else v_ref[j - CONV_BUF]
            term = w_ref[k:k + 1, :] * src
            acc = term if acc is None else acc + term
        y_ref[t] = _conv_epilogue(acc, b_ref, g_ref, be_ref)
    for j in range(CONV_BUF - steps):
        nst_ref[j] = st_ref[j + steps]
    for j in range(steps):
        nst_ref[CONV_BUF - steps + j] = v_ref[j]


def _conv_sample(v_tm, state_tm, w_dw, b_dw, ln_g, ln_b):
    steps, batch, _ = v_tm.shape
    bb = 32
    vec = pl.BlockSpec((1, D_CONV), lambda i: (0, 0))
    return pl.pallas_call(
        _conv_sample_kernel,
        grid=(batch // bb,),
        in_specs=[
            pl.BlockSpec((steps, bb, D_CONV), lambda i: (0, i, 0)),
            pl.BlockSpec((CONV_BUF, bb, D_CONV), lambda i: (0, i, 0)),
            pl.BlockSpec((CONV_WIDTH, D_CONV), lambda i: (0, 0)),
            vec, vec, vec,
        ],
        out_specs=[
            pl.BlockSpec((steps, bb, D_CONV), lambda i: (0, i, 0)),
            pl.BlockSpec((CONV_BUF, bb, D_CONV), lambda i: (0, i, 0)),
        ],
        out_shape=[jax.ShapeDtypeStruct((steps, batch, D_CONV), BF16),
                   jax.ShapeDtypeStruct((CONV_BUF, batch, D_CONV), F32)],
        compiler_params=_params(("parallel",)),
        name="conv_sample",
    )(v_tm, state_tm, w_dw, b_dw, ln_g, ln_b)


def _s5_params_kernel(lr_ref, li_ref, ld_ref, br_ref, bi_ref, cr_ref, ci_ref,
                      ar_ref, ai_ref, wb_ref, wc_ref):
    lr = lr_ref[...]
    li = li_ref[...]
    dt = jnp.exp(ld_ref[...])
    mag = jnp.exp(lr * dt)
    ar = mag * jnp.cos(li * dt)
    ai = mag * jnp.sin(li * dt)
    den = lr * lr + li * li
    cr = ((ar - 1) * lr + ai * li) / den
    ci = (ai * lr - (ar - 1) * li) / den
    ar_ref[...] = ar
    ai_ref[...] = ai

    br = br_ref[...]
    bi = bi_ref[...]
    shape_b = (S5_BLOCK_IN, S5_BLOCK_STATE)
    same_b = (jnp.right_shift(lax.broadcasted_iota(jnp.int32, shape_b, 0), LOG2_SSM_GROUP)
              == jnp.right_shift(lax.broadcasted_iota(jnp.int32, shape_b, 1), LOG2_SSM_STATE))
    wb_ref[:, 0:S5_BLOCK_STATE] = jnp.where(same_b, cr * br - ci * bi, 0.0).astype(BF16)
    wb_ref[:, S5_BLOCK_STATE:] = jnp.where(same_b, cr * bi + ci * br, 0.0).astype(BF16)

    shape_c = (S5_BLOCK_STATE, S5_BLOCK_IN)
    same_c = (jnp.right_shift(lax.broadcasted_iota(jnp.int32, shape_c, 0), LOG2_SSM_STATE)
              == jnp.right_shift(lax.broadcasted_iota(jnp.int32, shape_c, 1), LOG2_SSM_GROUP))
    wc_ref[0:S5_BLOCK_STATE, :] = jnp.where(same_c, cr_ref[...], 0.0).astype(BF16)
    wc_ref[S5_BLOCK_STATE:, :] = jnp.where(same_c, -ci_ref[...], 0.0).astype(BF16)


def _s5_params(lam_re, lam_im, log_dt, b_re, b_im, c_re, c_im):
    depth = lam_re.shape[0]
    nb, gpb = S5_BLOCKS, N_SSM_GROUPS // S5_BLOCKS

    def per_state(a):
        return a.reshape(depth, nb, 1, S5_BLOCK_STATE)

    ld = jnp.broadcast_to(log_dt[:, :, None], lam_re.shape)

    def b_layout(b):
        bt = jnp.transpose(b, (0, 1, 3, 2)).reshape(depth, nb, S5_BLOCK_IN, SSM_STATE)
        return jnp.tile(bt, (1, 1, 1, gpb))

    def c_layout(c):
        ct = c.reshape(depth, nb, gpb, SSM_GROUP, SSM_STATE)
        ct = jnp.transpose(ct, (0, 1, 4, 2, 3)).reshape(depth, nb, SSM_STATE, S5_BLOCK_IN)
        return jnp.tile(ct, (1, 1, gpb, 1))

    st = pl.BlockSpec((None, None, 1, S5_BLOCK_STATE), lambda l, r: (l, r, 0, 0))
    bs = pl.BlockSpec((None, None, S5_BLOCK_IN, S5_BLOCK_STATE), lambda l, r: (l, r, 0, 0))
    cs = pl.BlockSpec((None, None, S5_BLOCK_STATE, S5_BLOCK_IN), lambda l, r: (l, r, 0, 0))
    ar, ai, wb, wc = pl.pallas_call(
        _s5_params_kernel,
        grid=(depth, nb),
        in_specs=[st, st, st, bs, bs, cs, cs],
        out_specs=[
            st, st,
            pl.BlockSpec((None, None, S5_BLOCK_IN, 2 * S5_BLOCK_STATE), lambda l, r: (l, r, 0, 0)),
            pl.BlockSpec((None, None, 2 * S5_BLOCK_STATE, S5_BLOCK_IN), lambda l, r: (l, r, 0, 0)),
        ],
        out_shape=[
            jax.ShapeDtypeStruct((depth, nb, 1, S5_BLOCK_STATE), F32),
            jax.ShapeDtypeStruct((depth, nb, 1, S5_BLOCK_STATE), F32),
            jax.ShapeDtypeStruct((depth, nb, S5_BLOCK_IN, 2 * S5_BLOCK_STATE), BF16),
            jax.ShapeDtypeStruct((depth, nb, 2 * S5_BLOCK_STATE, S5_BLOCK_IN), BF16),
        ],
        compiler_params=_params(("parallel", "parallel")),
        name="s5_params",
    )(per_state(lam_re), per_state(lam_im), per_state(ld),
      b_layout(b_re), b_layout(b_im), c_layout(c_re), c_layout(c_im))
    return ar.reshape(depth, 1, N_STATE), ai.reshape(depth, 1, N_STATE), wb, wc


def _s5_output(y, u, d_ref, wglu_ref, bglu_ref):
    y = jax.nn.gelu(y + d_ref[...] * u)
    z = _dot(y.astype(BF16), wglu_ref[...]) + bglu_ref[...]
    return (y * jax.nn.sigmoid(z)).astype(BF16)


def _s5_prompt_kernel(u_ref, ar_ref, ai_ref, wb_ref, wc_ref, d_ref, wglu_ref, bglu_ref,
                      y_ref, sre_ref, sim_ref, xre, xim, cre, cim):
    ts = S5_TIME_TILE
    nc = S5_LANE_CHUNKS

    @pl.when(pl.program_id(1) == 0)
    def _():
        cre[...] = jnp.zeros_like(cre)
        cim[...] = jnp.zeros_like(cim)

    u = u_ref[...]
    ub = u.astype(BF16)
    for r in range(S5_BLOCKS):
        x = _dot(ub[:, r * S5_BLOCK_IN:(r + 1) * S5_BLOCK_IN], wb_ref[r])
        for c in range(nc):
            xre[c, pl.ds(r, ts, stride=S5_BLOCKS), :] = x[:, c * LANES:(c + 1) * LANES]
            xim[c, pl.ds(r, ts, stride=S5_BLOCKS), :] = x[:, S5_BLOCK_STATE + c * LANES:
                                                          S5_BLOCK_STATE + (c + 1) * LANES]

    a_r = [ar_ref[c] for c in range(nc)]
    a_i = [ai_ref[c] for c in range(nc)]

    def step(t, carry):
        srs, sis = carry
        row = pl.multiple_of(t * S5_BLOCKS, S5_BLOCKS)
        new_r, new_i = [], []
        for c in range(nc):
            rows = pl.ds(row, S5_BLOCKS)
            nr = a_r[c] * srs[c] - a_i[c] * sis[c] + xre[c, rows, :]
            ni = a_r[c] * sis[c] + a_i[c] * srs[c] + xim[c, rows, :]
            xre[c, rows, :] = nr
            xim[c, rows, :] = ni
            new_r.append(nr)
            new_i.append(ni)
        return tuple(new_r), tuple(new_i)

    carry0 = (tuple(cre[c] for c in range(nc)), tuple(cim[c] for c in range(nc)))
    srs, sis = lax.fori_loop(0, ts, step, carry0, unroll=4)
    for c in range(nc):
        cre[c] = srs[c]
        cim[c] = sis[c]

    cols = []
    for r in range(S5_BLOCKS):
        sr = jnp.concatenate([xre[c, pl.ds(r, ts, stride=S5_BLOCKS), :] for c in range(nc)], axis=1)
        si = jnp.concatenate([xim[c, pl.ds(r, ts, stride=S5_BLOCKS), :] for c in range(nc)], axis=1)
        cols.append(_dot(sr.astype(BF16), wc_ref[r, 0:S5_BLOCK_STATE, :])
                    + _dot(si.astype(BF16), wc_ref[r, S5_BLOCK_STATE:, :]))
    y_ref[...] = _s5_output(jnp.concatenate(cols, axis=1), u, d_ref, wglu_ref, bglu_ref)
    sre_ref[...] = cre[...]
    sim_ref[...] = cim[...]


def _to_slabs(a):
    lead = a.shape[:-1]
    a = a.reshape(*lead, S5_BLOCKS, S5_LANE_CHUNKS, LANES)
    return jnp.swapaxes(a, -3, -2)


def _from_slabs(a):
    lead = a.shape[:-3]
    return jnp.swapaxes(a, -3, -2).reshape(*lead, N_STATE)


def _s5_prompt(u, ar, ai, wb, wc, d, w_glu_bf16, b_glu, batch, seq):
    ts = S5_TIME_TILE
    nt = seq // ts
    slab = (S5_LANE_CHUNKS, S5_BLOCKS, LANES)
    full = lambda shape: pl.BlockSpec(shape, lambda b, t: (0,) * len(shape))
    y, sre, sim = pl.pallas_call(
        _s5_prompt_kernel,
        grid=(batch, nt),
        in_specs=[
            pl.BlockSpec((ts, D_SSM), lambda b, t: (b * nt + t, 0)),
            full(slab), full(slab),
            full(wb.shape), full(wc.shape),
            full((1, D_SSM)), full((D_SSM, D_SSM)), full((1, D_SSM)),
        ],
        out_specs=[
            pl.BlockSpec((ts, D_SSM), lambda b, t: (b * nt + t, 0)),
            pl.BlockSpec((None,) + slab, lambda b, t: (b, 0, 0, 0)),
            pl.BlockSpec((None,) + slab, lambda b, t: (b, 0, 0, 0)),
        ],
        out_shape=[jax.ShapeDtypeStruct((batch * seq, D_SSM), BF16),
                   jax.ShapeDtypeStruct((batch,) + slab, F32),
                   jax.ShapeDtypeStruct((batch,) + slab, F32)],
        scratch_shapes=[pltpu.VMEM((S5_LANE_CHUNKS, ts * S5_BLOCKS, LANES), F32),
                        pltpu.VMEM((S5_LANE_CHUNKS, ts * S5_BLOCKS, LANES), F32),
                        pltpu.VMEM(slab, F32), pltpu.VMEM(slab, F32)],
        compiler_params=_params(("parallel", "arbitrary")),
        name="s5_prompt",
    )(u, _to_slabs(ar[0]), _to_slabs(ai[0]), wb, wc, d, w_glu_bf16, b_glu)
    return y, _from_slabs(sre), _from_slabs(sim)


def _s5_sample_kernel(u_ref, s0r_ref, s0i_ref, ar_ref, ai_ref, wb_ref, wc_ref, d_ref,
                      wglu_ref, bglu_ref, y_ref, sre_ref, sim_ref, yscr):
    steps = u_ref.shape[0]
    for r in range(S5_BLOCKS):
        ls = slice(r * S5_BLOCK_STATE, (r + 1) * S5_BLOCK_STATE)
        cs = slice(r * S5_BLOCK_IN, (r + 1) * S5_BLOCK_IN)
        a_r = ar_ref[:, ls]
        a_i = ai_ref[:, ls]
        sr = s0r_ref[:, ls]
        si = s0i_ref[:, ls]
        for t in range(steps):
            x = _dot(u_ref[t, :, cs].astype(BF16), wb_ref[r])
            sr, si = (a_r * sr - a_i * si + x[:, :S5_BLOCK_STATE],
                      a_r * si + a_i * sr + x[:, S5_BLOCK_STATE:])
            yscr[t, :, cs] = (_dot(sr.astype(BF16), wc_ref[r, 0:S5_BLOCK_STATE, :])
                              + _dot(si.astype(BF16), wc_ref[r, S5_BLOCK_STATE:, :]))
        sre_ref[:, ls] = sr
        sim_ref[:, ls] = si
    for t in range(steps):
        y_ref[t] = _s5_output(yscr[t], u_ref[t], d_ref, wglu_ref, bglu_ref)


def _s5_sample(u_tm, s0r, s0i, ar, ai, wb, wc, d, w_glu_bf16, b_glu):
    steps, batch, _ = u_tm.shape
    return pl.pallas_call(
        _s5_sample_kernel,
        out_shape=[jax.ShapeDtypeStruct((steps, batch, D_SSM), BF16),
                   jax.ShapeDtypeStruct((batch, N_STATE), F32),
                   jax.ShapeDtypeStruct((batch, N_STATE), F32)],
        scratch_shapes=[pltpu.VMEM((steps, batch, D_SSM), F32)],
        compiler_params=pltpu.CompilerParams(vmem_limit_bytes=VMEM_LIMIT),
        name="s5_sample",
    )(u_tm, s0r, s0i, ar, ai, wb, wc, d, w_glu_bf16, b_glu)


def _layer_out_kernel(*refs, moe, yc_slabs):
    if moe:
        (yc_ref, ys_ref, wo_ref, x_ref, gt_ref, g_ref, sh_ref, sc_ref, wr_ref, br_ref,
         xo_ref, h_ref, lg_ref) = refs
    else:
        (yc_ref, ys_ref, wo_ref, x_ref, gt_ref, g_ref, sh_ref, sc_ref, xo_ref, h_ref) = refs
    tm = x_ref.shape[0]
    if yc_slabs:
        yc = jnp.concatenate([yc_ref[pl.ds(c, tm, stride=CONV_SUB), :] for c in range(CONV_SUB)],
                             axis=1).astype(BF16)
    else:
        yc = yc_ref[...]
    o = _dot(yc, wo_ref[0:D_CONV, :]) + _dot(ys_ref[...], wo_ref[D_CONV:, :])
    x = x_ref[...] + gt_ref[...] * o
    xo_ref[...] = x
    h = _rmsnorm(x, g_ref[...]) * (1 + sc_ref[...]) + sh_ref[...]
    if moe:
        h_ref[...] = h
        h_hi = h.astype(BF16)
        h_lo = (h - h_hi.astype(F32)).astype(BF16)
        wr = wr_ref[...]
        w_hi = wr.astype(BF16)
        w_lo = (wr - w_hi.astype(F32)).astype(BF16)
        lg_ref[...] = (_dot(h_hi, w_hi) + (_dot(h_lo, w_hi) + _dot(h_hi, w_lo))) + br_ref[...]
    else:
        h_ref[...] = h.astype(BF16)


def _layer_out(grp, layer, yc, ys, w_out_bf16, x, g_norm, router=None):
    tm = 256
    moe = router is not None
    row = lambda i: (i, 0)
    const = lambda i: (0, 0)
    if grp.v_slabs:
        yc = yc.reshape(grp.rows * CONV_SUB, LANES)
        yc_spec = pl.BlockSpec((tm * CONV_SUB, LANES), row)
    else:
        yc_spec = pl.BlockSpec((tm, D_CONV), row)
    in_specs = [
        yc_spec, pl.BlockSpec((tm, D_SSM), row),
        pl.BlockSpec((D_MODEL, D_MODEL), const),
        pl.BlockSpec((tm, D_MODEL), row),
        grp.mod_spec(layer, GATE_M, tm),
        pl.BlockSpec((1, D_MODEL), const),
        grp.mod_spec(layer, SHIFT_F, tm),
        grp.mod_spec(layer, SCALE_F, tm),
    ]
    args = [yc, ys, w_out_bf16, x, grp.mod, g_norm, grp.mod, grp.mod]
    out_specs = [pl.BlockSpec((tm, D_MODEL), row), pl.BlockSpec((tm, D_MODEL), row)]
    out_shape = [jax.ShapeDtypeStruct((grp.rows, D_MODEL), F32),
                 jax.ShapeDtypeStruct((grp.rows, D_MODEL), F32 if moe else BF16)]
    if moe:
        in_specs += [pl.BlockSpec((D_MODEL, LANES), const), pl.BlockSpec((1, LANES), const)]
        args += list(router)
        out_specs.append(pl.BlockSpec((tm, LANES), row))
        out_shape.append(jax.ShapeDtypeStruct((grp.rows, LANES), F32))
    return pl.pallas_call(
        functools.partial(_layer_out_kernel, moe=moe, yc_slabs=grp.v_slabs),
        grid=(grp.rows // tm,),
        in_specs=in_specs,
        out_specs=out_specs,
        out_shape=out_shape,
        compiler_params=_params(("parallel",)),
        name="layer_out",
    )(*args)


def _ffn_kernel(h_ref, wg_ref, wu_ref, wd_ref, o_ref):
    j = pl.program_id(1)
    h = h_ref[...]
    act = (jax.nn.silu(_dot(h, wg_ref[...])) * _dot(h, wu_ref[...])).astype(BF16)
    contrib = _dot(act, wd_ref[...])

    @pl.when(j == 0)
    def _():
        o_ref[...] = contrib

    @pl.when(j > 0)
    def _():
        o_ref[...] += contrib


def _ffn_dense(h, wg, wu, wd):
    rows = h.shape[0]
    tm = min(rows, FFN_ROW_TILE)
    tf = FFN_FF_TILE
    d_ff = wg.shape[1]
    return pl.pallas_call(
        _ffn_kernel,
        grid=(rows // tm, d_ff // tf),
        in_specs=[pl.BlockSpec((tm, D_MODEL), lambda i, j: (i, 0)),
                  pl.BlockSpec((D_MODEL, tf), lambda i, j: (0, j)),
                  pl.BlockSpec((D_MODEL, tf), lambda i, j: (0, j)),
                  pl.BlockSpec((tf, D_MODEL), lambda i, j: (j, 0))],
        out_specs=pl.BlockSpec((tm, D_MODEL), lambda i, j: (i, 0)),
        out_shape=jax.ShapeDtypeStruct((rows, D_MODEL), F32),
        compiler_params=_params(("parallel", "arbitrary")),
        name="ffn_dense",
    )(h, wg, wu, wd)


R_E1, R_E2, R_P1, R_P2, R_RANK1, R_RANK2 = range(6)


def _route_kernel(lp_ref, ls_ref, meta_ref, cnt_ref, carry, *, n_first):
    i = pl.program_id(0)
    tm = meta_ref.shape[0]

    @pl.when(i == 0)
    def _():
        carry[...] = jnp.zeros_like(carry)

    lane = lax.broadcasted_iota(jnp.int32, (tm, LANES), 1).astype(F32)
    logits = jnp.where(i < n_first, lp_ref[...], ls_ref[...])
    logits = jnp.where(lane < N_EXPERTS, logits, -jnp.inf)
    m1 = jnp.max(logits, axis=-1, keepdims=True)
    e1 = jnp.min(jnp.where(logits == m1, lane, float(LANES)), axis=-1, keepdims=True)
    rest = jnp.where(lane == e1, -jnp.inf, logits)
    m2 = jnp.max(rest, axis=-1, keepdims=True)
    e2 = jnp.min(jnp.where(rest == m2, lane, float(LANES)), axis=-1, keepdims=True)
    x2 = jnp.exp(m2 - m1)
    den = 1.0 + x2
    p1 = 1.0 / den
    p2 = x2 / den

    hot1 = lane == e1
    hot2 = lane == e2
    hot = jnp.logical_or(hot1, hot2).astype(F32)
    rows = lax.broadcasted_iota(jnp.int32, (tm, tm), 0)
    cols = lax.broadcasted_iota(jnp.int32, (tm, tm), 1)
    earlier = (cols < rows).astype(BF16)
    rank = _dot(earlier, hot.astype(BF16)) + carry[0:1, :]
    rank1 = jnp.sum(jnp.where(hot1, rank, 0.0), axis=-1, keepdims=True)
    rank2 = jnp.sum(jnp.where(hot2, rank, 0.0), axis=-1, keepdims=True)
    total = carry[0:1, :] + jnp.sum(hot, axis=0, keepdims=True)
    carry[...] = jnp.broadcast_to(total, carry.shape)
    cnt_ref[...] = jnp.broadcast_to(total, cnt_ref.shape)

    meta = jnp.where(lane == R_E1, e1.astype(F32), 0.0)
    meta = jnp.where(lane == R_E2, e2.astype(F32), meta)
    meta = jnp.where(lane == R_P1, p1, meta)
    meta = jnp.where(lane == R_P2, p2, meta)
    meta = jnp.where(lane == R_RANK1, rank1, meta)
    meta = jnp.where(lane == R_RANK2, rank2, meta)
    meta_ref[...] = meta


def _route(lg_p, lg_s):
    tm = ROW_TILE
    n_first = lg_p.shape[0] // tm
    n_tiles = n_first + lg_s.shape[0] // tm
    first = lambda i: (jnp.minimum(i, n_first - 1), 0)
    second = lambda i: (jnp.maximum(i - n_first, 0), 0)
    return pl.pallas_call(
        functools.partial(_route_kernel, n_first=n_first),
        grid=(n_tiles,),
        in_specs=[pl.BlockSpec((tm, LANES), first), pl.BlockSpec((tm, LANES), second)],
        out_specs=[pl.BlockSpec((tm, LANES), lambda i: (i, 0)),
                   pl.BlockSpec((8, LANES), lambda i: (0, 0))],
        out_shape=[jax.ShapeDtypeStruct((n_tiles * tm, LANES), F32),
                   jax.ShapeDtypeStruct((8, LANES), F32)],
        scratch_shapes=[pltpu.VMEM((8, LANES), F32)],
        compiler_params=_params(("arbitrary",)),
        name="moe_route",
    )(lg_p, lg_s)


def _row_copies(n, start_fn):
    def issue(r, c):
        for cp in start_fn(r):
            cp.start()
        return c
    lax.fori_loop(0, n, issue, 0, unroll=4)

    def drain(r, c):
        for cp in start_fn(0):
            cp.wait()
        return c
    lax.fori_loop(0, n, drain, 0, unroll=4)


def _dispatch_kernel(p1_ref, p2_ref, hp_ref, hs_ref, xs_in_ref, xs_ref, sem, *, n_first):
    del xs_in_ref
    i = pl.program_id(0)
    tm = hp_ref.shape[0]
    base = i * tm

    def run(src_ref):
        def copies(r):
            row = src_ref.at[pl.ds(r, 1), :]
            return (pltpu.make_async_copy(row, xs_ref.at[pl.ds(p1_ref[base + r], 1), :], sem.at[0]),
                    pltpu.make_async_copy(row, xs_ref.at[pl.ds(p2_ref[base + r], 1), :], sem.at[1]))
        _row_copies(tm, copies)

    @pl.when(i < n_first)
    def _():
        run(hp_ref)

    @pl.when(i >= n_first)
    def _():
        run(hs_ref)


def _dispatch(pos1, pos2, h_p, h_s, n_slots):
    tm = ROW_TILE
    n_first = h_p.shape[0] // tm
    n_tiles = n_first + h_s.shape[0] // tm
    first = lambda i, p1, p2: (jnp.minimum(i, n_first - 1), 0)
    second = lambda i, p1, p2: (jnp.maximum(i - n_first, 0), 0)
    xs0 = jnp.zeros((n_slots, D_MODEL), F32)
    return pl.pallas_call(
        functools.partial(_dispatch_kernel, n_first=n_first),
        grid_spec=pltpu.PrefetchScalarGridSpec(
            num_scalar_prefetch=2,
            grid=(n_tiles,),
            in_specs=[pl.BlockSpec((tm, D_MODEL), first), pl.BlockSpec((tm, D_MODEL), second),
                      pl.BlockSpec(memory_space=pl.ANY)],
            out_specs=pl.BlockSpec(memory_space=pl.ANY),
            scratch_shapes=[pltpu.SemaphoreType.DMA((2,))],
        ),
        out_shape=jax.ShapeDtypeStruct((n_slots, D_MODEL), F32),
        input_output_aliases={4: 0},
        compiler_params=_params(("arbitrary",)),
        name="moe_dispatch",
    )(pos1, pos2, h_p, h_s, xs0)


def _moe_kernel(te_ref, nv_ref, xs_ref, wg_ref, wu_ref, wd_ref, o_ref, h_scr):
    i = pl.program_id(0)
    j = pl.program_id(1)
    n_sub = (nv_ref[i] + (MOE_SUB_ROWS - 1)) // MOE_SUB_ROWS

    @pl.when(j == 0)
    def _():
        o_ref[...] = jnp.zeros_like(o_ref)
        h_scr[...] = xs_ref[...].astype(BF16)

    for m in range(1, MOE_ROW_TILE // MOE_SUB_ROWS + 1):
        @pl.when(n_sub == m)
        def _(m=m):
            rows = slice(0, m * MOE_SUB_ROWS)
            h = h_scr[rows, :]
            act = (jax.nn.silu(_dot(h, wg_ref[...].astype(BF16)))
                   * _dot(h, wu_ref[...].astype(BF16))).astype(BF16)
            o_ref[rows, :] += _dot(act, wd_ref[...].astype(BF16))


def _moe_experts(tile_expert, n_valid, xs, wg, wu, wd):
    tm, tf = MOE_ROW_TILE, MOE_FF_TILE
    n_tiles = xs.shape[0] // tm
    d_ff = wg.shape[2]
    nj = d_ff // tf

    def ff(i, j, te, nv):
        return jnp.where(nv[i] > 0, j, nj - 1)

    return pl.pallas_call(
        _moe_kernel,
        grid_spec=pltpu.PrefetchScalarGridSpec(
            num_scalar_prefetch=2,
            grid=(n_tiles, nj),
            in_specs=[
                pl.BlockSpec((tm, D_MODEL), lambda i, j, te, nu: (i, 0)),
                pl.BlockSpec((None, D_MODEL, tf), lambda i, j, te, nu: (te[i], 0, ff(i, j, te, nu))),
                pl.BlockSpec((None, D_MODEL, tf), lambda i, j, te, nu: (te[i], 0, ff(i, j, te, nu))),
                pl.BlockSpec((None, tf, D_MODEL), lambda i, j, te, nu: (te[i], ff(i, j, te, nu), 0)),
            ],
            out_specs=pl.BlockSpec((tm, D_MODEL), lambda i, j, te, nu: (i, 0)),
            scratch_shapes=[pltpu.VMEM((tm, D_MODEL), BF16)],
        ),
        out_shape=jax.ShapeDtypeStruct(xs.shape, F32),
        compiler_params=_params(("arbitrary", "arbitrary")),
        name="moe_experts",
    )(tile_expert, n_valid, xs, wg, wu, wd)


def _combine_kernel(p1_ref, p2_ref, ys_ref, meta_ref, o_ref, y1, y2, sem):
    tm = o_ref.shape[0]
    base = pl.program_id(0) * tm

    def copies(r):
        return (pltpu.make_async_copy(ys_ref.at[pl.ds(p1_ref[base + r], 1), :],
                                      y1.at[pl.ds(r, 1), :], sem.at[0]),
                pltpu.make_async_copy(ys_ref.at[pl.ds(p2_ref[base + r], 1), :],
                                      y2.at[pl.ds(r, 1), :], sem.at[1]))
    _row_copies(tm, copies)
    meta = meta_ref[...]
    o_ref[...] = meta[:, R_P1:R_P1 + 1] * y1[...] + meta[:, R_P2:R_P2 + 1] * y2[...]


def _combine(pos1, pos2, ys, meta):
    tm = ROW_TILE
    n_rows = meta.shape[0]
    return pl.pallas_call(
        _combine_kernel,
        grid_spec=pltpu.PrefetchScalarGridSpec(
            num_scalar_prefetch=2,
            grid=(n_rows // tm,),
            in_specs=[pl.BlockSpec(memory_space=pl.ANY),
                      pl.BlockSpec((tm, LANES), lambda i, p1, p2: (i, 0))],
            out_specs=pl.BlockSpec((tm, D_MODEL), lambda i, p1, p2: (i, 0)),
            scratch_shapes=[pltpu.VMEM((tm, D_MODEL), F32), pltpu.VMEM((tm, D_MODEL), F32),
                            pltpu.SemaphoreType.DMA((2,))],
        ),
        out_shape=jax.ShapeDtypeStruct((n_rows, D_MODEL), F32),
        compiler_params=_params(("arbitrary",)),
        name="moe_combine",
    )(pos1, pos2, ys, meta)


def _moe(h_p, h_s, lg_p, lg_s, wg, wu, wd):
    tm = MOE_ROW_TILE
    n_rows = h_p.shape[0] + h_s.shape[0]
    max_tiles = (2 * n_rows) // tm + N_EXPERTS
    meta, cnt = _route(lg_p, lg_s)
    counts = cnt[0, :N_EXPERTS].astype(jnp.int32)
    tiles = (counts + tm - 1) // tm
    ends = jnp.cumsum(tiles)
    seg_start = (ends - tiles) * tm
    e1 = meta[:, R_E1].astype(jnp.int32)
    e2 = meta[:, R_E2].astype(jnp.int32)
    pos1 = seg_start[e1] + meta[:, R_RANK1].astype(jnp.int32)
    pos2 = seg_start[e2] + meta[:, R_RANK2].astype(jnp.int32)
    tile_ids = jnp.arange(max_tiles, dtype=jnp.int32)
    used = tile_ids < ends[-1]
    tile_expert = jnp.sum(jnp.minimum(tile_ids, ends[-1] - 1)[:, None] >= ends[None, :],
                          axis=1).astype(jnp.int32)
    seg_end = seg_start[tile_expert] + counts[tile_expert]
    n_valid = jnp.where(used, jnp.clip(seg_end - tile_ids * tm, 0, tm), 0).astype(jnp.int32)

    xs = _dispatch(pos1, pos2, h_p, h_s, max_tiles * tm)
    ys = _moe_experts(tile_expert, n_valid, xs, wg, wu, wd)
    return _combine(pos1, pos2, ys, meta)


def _final_kernel(x_ref, f_ref, gt_ref, g_ref, o_ref):
    x = x_ref[...] + gt_ref[...] * f_ref[...]
    o_ref[...] = _rmsnorm(x, g_ref[...])


def _final(grp, layer, x, f_prev, g_final):
    tm = ROW_TILE
    f, off = f_prev
    return pl.pallas_call(
        _final_kernel,
        grid=(grp.rows // tm,),
        in_specs=[pl.BlockSpec((tm, D_MODEL), lambda i: (i, 0)),
                  pl.BlockSpec((tm, D_MODEL), lambda i: (i + off, 0)),
                  grp.mod_spec(layer, GATE_F, tm),
                  pl.BlockSpec((1, D_MODEL), lambda i: (0, 0))],
        out_specs=pl.BlockSpec((tm, D_MODEL), lambda i: (i, 0)),
        out_shape=jax.ShapeDtypeStruct((grp.rows, D_MODEL), F32),
        compiler_params=_params(("parallel",)),
        name="final_norm",
    )(x, f, grp.mod, g_final)


def kernel(x_prompt, x_sample, c_prompt, c_sample, state_conv, state_ssm_re, state_ssm_im, w_ada, b_ada, g_norm_mix, g_norm_ffn, w_in, w_dw, b_dw, g_ln_conv, b_ln_conv, lam_re, lam_im, log_dt, b_ssm_re, b_ssm_im, c_ssm_re, c_ssm_im, d_ssm, w_glu, b_glu, w_out, w_gate_dense, w_up_dense, w_down_dense, w_router, b_router, w_gate_exp, w_up_exp, w_down_exp, g_final):
    depth = w_in.shape[0]
    bp, seq, _ = x_prompt.shape
    bs, steps, _ = x_sample.shape
    rows_p, rows_s = bp * seq, bs * steps
    assert rows_p % ROW_TILE == 0 and rows_s == ROW_TILE and seq % ROW_TILE == 0

    n_c = bp + bs
    pad = (-n_c) % 8
    c_all = jnp.concatenate([c_prompt, c_sample, jnp.zeros((pad, D_MODEL), F32)], axis=0)
    mod = _ada(c_all, w_ada, b_ada)
    mod_p = mod[:, :bp].reshape(depth, bp, 1, 6 * D_MODEL)
    mod_s = jnp.tile(mod[:, bp:n_c], (1, steps, 1))
    grp_p = _Group(rows_p, seq, mod_p, 0, True)
    grp_s = _Group(rows_s, None, mod_s, rows_p // ROW_TILE, False)

    ar, ai, wb, wc = _s5_params(lam_re, lam_im, log_dt, b_ssm_re, b_ssm_im, c_ssm_re, c_ssm_im)

    w_in_b = [_cast_bf16(w_in, l, 512) for l in range(depth)]
    w_glu_b = [_cast_bf16(w_glu, l, 512) for l in range(depth)]
    w_out_b = [_cast_bf16(w_out, l, 512) for l in range(depth)]
    n_dense = w_gate_dense.shape[0]
    wg_d = [_cast_bf16(w_gate_dense, i, 256) for i in range(n_dense)]
    wu_d = [_cast_bf16(w_up_dense, i, 256) for i in range(n_dense)]
    wd_d = [_cast_bf16(w_down_dense, i, 512) for i in range(n_dense)]

    x_p = x_prompt.reshape(rows_p, D_MODEL)
    x_s = jnp.transpose(x_sample, (1, 0, 2)).reshape(rows_s, D_MODEL)
    conv_tm = jnp.transpose(state_conv, (0, 2, 1, 3))
    row1 = lambda a: a.reshape(1, -1)

    f_p = f_s = None
    new_conv_p, new_re_p, new_im_p, new_conv_s, new_re_s, new_im_s = [], [], [], [], [], []
    for l in range(depth):
        g_mix, g_ffn = row1(g_norm_mix[l]), row1(g_norm_ffn[l])
        conv_w = (w_dw[l], row1(b_dw[l]), row1(g_ln_conv[l]), row1(b_ln_conv[l]))
        s5_w = (ar[l], ai[l], wb[l], wc[l], row1(d_ssm[l]), w_glu_b[l], row1(b_glu[l]))

        x_p, v_p, u_p = _layer_in(grp_p, l, x_p, f_p, g_mix, w_in_b[l])
        x_s, v_s, u_s = _layer_in(grp_s, l, x_s, f_s, g_mix, w_in_b[l])

        yc_p, cst_p = _conv_prompt(v_p, *conv_w, bp, seq)
        yc_s, cst_s = _conv_sample(v_s.reshape(steps, bs, D_CONV), conv_tm[l], *conv_w)
        ys_p, sre_p, sim_p = _s5_prompt(u_p, *s5_w, bp, seq)
        ys_s, sre_s, sim_s = _s5_sample(u_s.reshape(steps, bs, D_SSM),
                                        state_ssm_re[l].reshape(bs, N_STATE),
                                        state_ssm_im[l].reshape(bs, N_STATE), *s5_w)
        new_conv_p.append(cst_p)
        new_re_p.append(sre_p.reshape(bp, N_SSM_GROUPS, SSM_STATE))
        new_im_p.append(sim_p.reshape(bp, N_SSM_GROUPS, SSM_STATE))
        new_conv_s.append(jnp.transpose(cst_s, (1, 0, 2)))
        new_re_s.append(sre_s.reshape(bs, N_SSM_GROUPS, SSM_STATE))
        new_im_s.append(sim_s.reshape(bs, N_SSM_GROUPS, SSM_STATE))

        yc_s = yc_s.reshape(rows_s, D_CONV)
        ys_s = ys_s.reshape(rows_s, D_SSM)
        i = l // 2
        if l % 2 == 0:
            x_p, h_p = _layer_out(grp_p, l, yc_p, ys_p, w_out_b[l], x_p, g_ffn)
            x_s, h_s = _layer_out(grp_s, l, yc_s, ys_s, w_out_b[l], x_s, g_ffn)
            f_p = (_ffn_dense(h_p, wg_d[i], wu_d[i], wd_d[i]), 0)
            f_s = (_ffn_dense(h_s, wg_d[i], wu_d[i], wd_d[i]), 0)
        else:
            router = (jnp.pad(w_router[i], ((0, 0), (0, LANES - N_EXPERTS))),
                      jnp.pad(b_router[i], (0, LANES - N_EXPERTS)).reshape(1, LANES))
            x_p, h_p, lg_p = _layer_out(grp_p, l, yc_p, ys_p, w_out_b[l], x_p, g_ffn, router)
            x_s, h_s, lg_s = _layer_out(grp_s, l, yc_s, ys_s, w_out_b[l], x_s, g_ffn, router)
            f = _moe(h_p, h_s, lg_p, lg_s, w_gate_exp[i], w_up_exp[i], w_down_exp[i])
            f_p, f_s = (f, grp_p.row_block_offset), (f, grp_s.row_block_offset)

    g_fin = row1(g_final)
    y_p = _final(grp_p, depth - 1, x_p, f_p, g_fin).reshape(bp, seq, D_MODEL)
    y_s = _final(grp_s, depth - 1, x_s, f_s, g_fin).reshape(steps, bs, D_MODEL)
    y_s = jnp.transpose(y_s, (1, 0, 2))
    return (y_p, y_s, jnp.stack(new_conv_p), jnp.stack(new_re_p), jnp.stack(new_im_p),
            jnp.stack(new_conv_s), jnp.stack(new_re_s), jnp.stack(new_im_s))
```

```python
import functools

import jax
import jax.numpy as jnp
from jax import lax
from jax.experimental import pallas as pl
from jax.experimental.pallas import tpu as pltpu

F32 = jnp.float32
BF16 = jnp.bfloat16

D_MODEL = 2048
D_CONV = 1024
D_SSM = 1024
CONV_WIDTH = 31
CONV_BUF = CONV_WIDTH - 1
SSM_GROUP = 16
N_SSM_GROUPS = 64
SSM_STATE = 64
LOG2_SSM_GROUP = 4
LOG2_SSM_STATE = 6
N_STATE = N_SSM_GROUPS * SSM_STATE
N_EXPERTS = 8
EPS = 1e-6

S5_BLOCKS = 8
S5_BLOCK_IN = D_SSM // S5_BLOCKS
S5_BLOCK_STATE = N_STATE // S5_BLOCKS
S5_LANE_CHUNKS = S5_BLOCK_STATE // 128
SUBLANES = 8
CONV_SUB = D_CONV // 128

LANES = 128
VMEM_LIMIT = 56 * 1024 * 1024

ROW_TILE = 512
MOE_ROW_TILE = 1024
MOE_SUB_ROWS = 128
MOE_FF_TILE = 256
FFN_ROW_TILE = 1024
FFN_FF_TILE = 512
CONV_TIME_TILE = 256
CONV_TIME_CHUNK = 16
S5_TIME_TILE = 256

SHIFT_M, SCALE_M, GATE_M, SHIFT_F, SCALE_F, GATE_F = range(6)


def _params(semantics, vmem=VMEM_LIMIT):
    return pltpu.CompilerParams(dimension_semantics=semantics, vmem_limit_bytes=vmem)


def _dot(a, b):
    return jnp.dot(a, b, preferred_element_type=F32)


def _rmsnorm(x, g):
    return x * lax.rsqrt(jnp.mean(x * x, axis=-1, keepdims=True) + EPS) * g


def _layernorm(x, g, b):
    xc = x - jnp.mean(x, axis=-1, keepdims=True)
    var = jnp.mean(xc * xc, axis=-1, keepdims=True)
    return xc * lax.rsqrt(var + EPS) * g + b


def _cast_kernel(x_ref, o_ref):
    o_ref[...] = x_ref[...].astype(o_ref.dtype)


def _cast_bf16(x, layer, rows):
    _, r, c = x.shape
    return pl.pallas_call(
        _cast_kernel,
        grid=(r // rows,),
        in_specs=[pl.BlockSpec((None, rows, c), lambda i: (layer, i, 0))],
        out_specs=pl.BlockSpec((rows, c), lambda i: (i, 0)),
        out_shape=jax.ShapeDtypeStruct((r, c), BF16),
        compiler_params=_params(("parallel",)),
        name="cast_bf16",
    )(x)


def _ada_kernel(c_ref, w_ref, b_ref, o_ref):
    cs = jax.nn.silu(c_ref[...]).astype(BF16)
    o_ref[...] = _dot(cs, w_ref[...].astype(BF16)) + b_ref[...]


def _ada(c_all, w_ada, b_ada):
    depth, _, n = w_ada.shape
    rows = c_all.shape[0]
    tn = 1024
    return pl.pallas_call(
        _ada_kernel,
        grid=(depth, n // tn),
        in_specs=[
            pl.BlockSpec((rows, D_MODEL), lambda l, j: (0, 0)),
            pl.BlockSpec((None, D_MODEL, tn), lambda l, j: (l, 0, j)),
            pl.BlockSpec((None, 1, tn), lambda l, j: (l, 0, j)),
        ],
        out_specs=pl.BlockSpec((None, rows, tn), lambda l, j: (l, 0, j)),
        out_shape=jax.ShapeDtypeStruct((depth, rows, n), F32),
        compiler_params=_params(("parallel", "parallel")),
        name="ada_mod",
    )(c_all, w_ada, b_ada.reshape(depth, 1, n))


class _Group:
    def __init__(self, rows, rows_per_batch, mod, row_block_offset, v_slabs):
        self.rows = rows
        self.rows_per_batch = rows_per_batch
        self.mod = mod
        self.row_block_offset = row_block_offset
        self.v_slabs = v_slabs

    def mod_spec(self, layer, chunk, tm):
        if self.rows_per_batch is None:
            def imap(*ids):
                return (layer, ids[0], chunk)
            return pl.BlockSpec((None, tm, D_MODEL), imap)
        per = self.rows_per_batch // tm

        def imap(*ids):
            return (layer, ids[0] // per, 0, chunk)
        return pl.BlockSpec((None, None, 1, D_MODEL), imap)


def _layer_in_kernel(*refs, has_prev, v_slabs):
    if has_prev:
        (x_ref, f_ref, gtf_ref, g_ref, sh_ref, sc_ref, w_ref, xo_ref, v_ref, u_ref) = refs
    else:
        (x_ref, g_ref, sh_ref, sc_ref, w_ref, v_ref, u_ref) = refs
    tm = x_ref.shape[0]
    x = x_ref[...]
    if has_prev:
        x = x + gtf_ref[...] * f_ref[...]
        xo_ref[...] = x
    h = (_rmsnorm(x, g_ref[...]) * (1 + sc_ref[...]) + sh_ref[...]).astype(BF16)
    tn = 512
    for j in range(D_CONV // tn):
        cols = slice(j * tn, (j + 1) * tn)
        a = _dot(h, w_ref[:, cols])
        g = _dot(h, w_ref[:, D_CONV + j * tn:D_CONV + (j + 1) * tn])
        v = a * jax.nn.sigmoid(g)
        if v_slabs:
            for c in range(tn // LANES):
                v_ref[pl.ds(j * (tn // LANES) + c, tm, stride=CONV_SUB), :] = v[:, c * LANES:(c + 1) * LANES]
        else:
            v_ref[:, cols] = v
        u_ref[:, cols] = _dot(h, w_ref[:, 2 * D_CONV + j * tn:2 * D_CONV + (j + 1) * tn])


def _layer_in(grp, layer, x, f_prev, g_norm, w_in_bf16):
    tm = 256
    has_prev = f_prev is not None
    row = lambda i: (i, 0)
    const = lambda i: (0, 0)
    in_specs = [pl.BlockSpec((tm, D_MODEL), row)]
    args = [x]
    if has_prev:
        f_arr, f_off = f_prev
        off = f_off * (ROW_TILE // tm)
        in_specs += [pl.BlockSpec((tm, D_MODEL), lambda i: (i + off, 0)),
                     grp.mod_spec(layer - 1, GATE_F, tm)]
        args += [f_arr, grp.mod]
    in_specs += [
        pl.BlockSpec((1, D_MODEL), const),
        grp.mod_spec(layer, SHIFT_M, tm),
        grp.mod_spec(layer, SCALE_M, tm),
        pl.BlockSpec(w_in_bf16.shape, const),
    ]
    args += [g_norm, grp.mod, grp.mod, w_in_bf16]
    if grp.v_slabs:
        v_spec = pl.BlockSpec((tm * CONV_SUB, LANES), row)
        v_shape = jax.ShapeDtypeStruct((grp.rows * CONV_SUB, LANES), F32)
    else:
        v_spec = pl.BlockSpec((tm, D_CONV), row)
        v_shape = jax.ShapeDtypeStruct((grp.rows, D_CONV), F32)
    out_specs = [v_spec, pl.BlockSpec((tm, D_SSM), row)]
    out_shape = [v_shape, jax.ShapeDtypeStruct((grp.rows, D_SSM), F32)]
    if has_prev:
        out_specs = [pl.BlockSpec((tm, D_MODEL), row)] + out_specs
        out_shape = [jax.ShapeDtypeStruct((grp.rows, D_MODEL), F32)] + out_shape
    outs = pl.pallas_call(
        functools.partial(_layer_in_kernel, has_prev=has_prev, v_slabs=grp.v_slabs),
        grid=(grp.rows // tm,),
        in_specs=in_specs,
        out_specs=out_specs,
        out_shape=out_shape,
        compiler_params=_params(("parallel",)),
        name="layer_in",
    )(*args)
    outs = list(outs) if has_prev else [x] + list(outs)
    if grp.v_slabs:
        outs[1] = outs[1].reshape(grp.rows, CONV_SUB, LANES)
    return outs


def _conv_epilogue(acc, b_ref, g_ref, be_ref):
    y = _layernorm(acc + b_ref[...], g_ref[...], be_ref[...])
    return jax.nn.silu(y).astype(BF16)


def _conv_prompt_kernel(v_ref, w_ref, b_ref, g_ref, be_ref, y_ref, st_ref, buf):
    tt = CONV_TIME_TILE
    tc = CONV_TIME_CHUNK
    t = pl.program_id(1)

    @pl.when(t == 0)
    def _():
        buf[0:32] = jnp.zeros((32, CONV_SUB, LANES), F32)

    @pl.when(t > 0)
    def _():
        buf[0:32] = buf[tt:tt + 32]

    buf[32:32 + tt] = v_ref[...]

    def chunk(ci, carry):
        t0 = ci * tc
        acc = None
        for k in range(CONV_WIDTH):
            term = w_ref[k] * buf[pl.ds(t0 + 2 + k, tc)]
            acc = term if acc is None else acc + term
        y = acc + b_ref[...]

        def mean_c(a):
            return jnp.sum(jnp.sum(a, axis=2, keepdims=True), axis=1, keepdims=True) * (1.0 / D_CONV)
        yc = y - mean_c(y)
        yn = yc * lax.rsqrt(mean_c(yc * yc) + EPS) * g_ref[...] + be_ref[...]
        y_ref[pl.ds(t0, tc)] = jax.nn.silu(yn)
        return carry

    lax.fori_loop(0, tt // tc, chunk, 0, unroll=2)
    st_ref[...] = buf[32 + tt - CONV_BUF:32 + tt]


def _conv_prompt(v, w_dw, b_dw, ln_g, ln_b, batch, seq):
    tt = CONV_TIME_TILE
    nt = seq // tt
    tile = (CONV_SUB, LANES)
    vec = pl.BlockSpec(tile, lambda b, t: (0, 0))
    y, st = pl.pallas_call(
        _conv_prompt_kernel,
        grid=(batch, nt),
        in_specs=[
            pl.BlockSpec((tt,) + tile, lambda b, t: (b * nt + t, 0, 0)),
            pl.BlockSpec((CONV_WIDTH,) + tile, lambda b, t: (0, 0, 0)),
            vec, vec, vec,
        ],
        out_specs=[
            pl.BlockSpec((tt,) + tile, lambda b, t: (b * nt + t, 0, 0)),
            pl.BlockSpec((None, CONV_BUF) + tile, lambda b, t: (b, 0, 0, 0)),
        ],
        out_shape=[jax.ShapeDtypeStruct((batch * seq,) + tile, F32),
                   jax.ShapeDtypeStruct((batch, CONV_BUF) + tile, F32)],
        scratch_shapes=[pltpu.VMEM((32 + tt,) + tile, F32)],
        compiler_params=_params(("parallel", "arbitrary")),
        name="conv_prompt",
    )(v, w_dw.reshape((CONV_WIDTH,) + tile), b_dw.reshape(tile), ln_g.reshape(tile), ln_b.reshape(tile))
    return y, st.reshape(batch, CONV_BUF, D_CONV)


def _conv_sample_kernel(v_ref, st_ref, w_ref, b_ref, g_ref, be_ref, y_ref, nst_ref):
    steps = v_ref.shape[0]
    for t in range(steps):
        acc = None
        for k in range(CONV_WIDTH):
            j = t + k
            src = st_ref[j] if j < CONV_BUF else v_ref[j - CONV_BUF]
            term = w_ref[k:k + 1, :] * src
            acc = term if acc is None else acc + term
        y_ref[t] = _conv_epilogue(acc, b_ref, g_ref, be_ref)
    for j in range(CONV_BUF - steps):
        nst_ref[j] = st_ref[j + steps]
    for j in range(steps):
        nst_ref[CONV_BUF - steps + j] = v_ref[j]


def _conv_sample(v_tm, state_tm, w_dw, b_dw, ln_g, ln_b):
    steps, batch, _ = v_tm.shape
    bb = 32
    vec = pl.BlockSpec((1, D_CONV), lambda i: (0, 0))
    return pl.pallas_call(
        _conv_sample_kernel,
        grid=(batch // bb,),
        in_specs=[
            pl.BlockSpec((steps, bb, D_CONV), lambda i: (0, i, 0)),
            pl.BlockSpec((CONV_BUF, bb, D_CONV), lambda i: (0, i, 0)),
            pl.BlockSpec((CONV_WIDTH, D_CONV), lambda i: (0, 0)),
            vec, vec, vec,
        ],
        out_specs=[
            pl.BlockSpec((steps, bb, D_CONV), lambda i: (0, i, 0)),
            pl.BlockSpec((CONV_BUF, bb, D_CONV), lambda i: (0, i, 0)),
        ],
        out_shape=[jax.ShapeDtypeStruct((steps, batch, D_CONV), BF16),
                   jax.ShapeDtypeStruct((CONV_BUF, batch, D_CONV), F32)],
        compiler_params=_params(("parallel",)),
        name="conv_sample",
    )(v_tm, state_tm, w_dw, b_dw, ln_g, ln_b)


def _s5_params_kernel(lr_ref, li_ref, ld_ref, br_ref, bi_ref, cr_ref, ci_ref,
                      ar_ref, ai_ref, wb_ref, wc_ref):
    lr = lr_ref[...]
    li = li_ref[...]
    dt = jnp.exp(ld_ref[...])
    mag = jnp.exp(lr * dt)
    ar = mag * jnp.cos(li * dt)
    ai = mag * jnp.sin(li * dt)
    den = lr * lr + li * li
    cr = ((ar - 1) * lr + ai * li) / den
    ci = (ai * lr - (ar - 1) * li) / den
    ar_ref[...] = ar
    ai_ref[...] = ai

    br = br_ref[...]
    bi = bi_ref[...]
    shape_b = (S5_BLOCK_IN, S5_BLOCK_STATE)
    same_b = (jnp.right_shift(lax.broadcasted_iota(jnp.int32, shape_b, 0), LOG2_SSM_GROUP)
              == jnp.right_shift(lax.broadcasted_iota(jnp.int32, shape_b, 1), LOG2_SSM_STATE))
    wb_ref[:, 0:S5_BLOCK_STATE] = jnp.where(same_b, cr * br - ci * bi, 0.0).astype(BF16)
    wb_ref[:, S5_BLOCK_STATE:] = jnp.where(same_b, cr * bi + ci * br, 0.0).astype(BF16)

    shape_c = (S5_BLOCK_STATE, S5_BLOCK_IN)
    same_c = (jnp.right_shift(lax.broadcasted_iota(jnp.int32, shape_c, 0), LOG2_SSM_STATE)
              == jnp.right_shift(lax.broadcasted_iota(jnp.int32, shape_c, 1), LOG2_SSM_GROUP))
    wc_ref[0:S5_BLOCK_STATE, :] = jnp.where(same_c, cr_ref[...], 0.0).astype(BF16)
    wc_ref[S5_BLOCK_STATE:, :] = jnp.where(same_c, -ci_ref[...], 0.0).astype(BF16)


def _s5_params(lam_re, lam_im, log_dt, b_re, b_im, c_re, c_im):
    depth = lam_re.shape[0]
    nb, gpb = S5_BLOCKS, N_SSM_GROUPS // S5_BLOCKS

    def per_state(a):
        return a.reshape(depth, nb, 1, S5_BLOCK_STATE)

    ld = jnp.broadcast_to(log_dt[:, :, None], lam_re.shape)

    def b_layout(b):
        bt = jnp.transpose(b, (0, 1, 3, 2)).reshape(depth, nb, S5_BLOCK_IN, SSM_STATE)
        return jnp.tile(bt, (1, 1, 1, gpb))

    def c_layout(c):
        ct = c.reshape(depth, nb, gpb, SSM_GROUP, SSM_STATE)
        ct = jnp.transpose(ct, (0, 1, 4, 2, 3)).reshape(depth, nb, SSM_STATE, S5_BLOCK_IN)
        return jnp.tile(ct, (1, 1, gpb, 1))

    st = pl.BlockSpec((None, None, 1, S5_BLOCK_STATE), lambda l, r: (l, r, 0, 0))
    bs = pl.BlockSpec((None, None, S5_BLOCK_IN, S5_BLOCK_STATE), lambda l, r: (l, r, 0, 0))
    cs = pl.BlockSpec((None, None, S5_BLOCK_STATE, S5_BLOCK_IN), lambda l, r: (l, r, 0, 0))
    ar, ai, wb, wc = pl.pallas_call(
        _s5_params_kernel,
        grid=(depth, nb),
        in_specs=[st, st, st, bs, bs, cs, cs],
        out_specs=[
            st, st,
            pl.BlockSpec((None, None, S5_BLOCK_IN, 2 * S5_BLOCK_STATE), lambda l, r: (l, r, 0, 0)),
            pl.BlockSpec((None, None, 2 * S5_BLOCK_STATE, S5_BLOCK_IN), lambda l, r: (l, r, 0, 0)),
        ],
        out_shape=[
            jax.ShapeDtypeStruct((depth, nb, 1, S5_BLOCK_STATE), F32),
            jax.ShapeDtypeStruct((depth, nb, 1, S5_BLOCK_STATE), F32),
            jax.ShapeDtypeStruct((depth, nb, S5_BLOCK_IN, 2 * S5_BLOCK_STATE), BF16),
            jax.ShapeDtypeStruct((depth, nb, 2 * S5_BLOCK_STATE, S5_BLOCK_IN), BF16),
        ],
        compiler_params=_params(("parallel", "parallel")),
        name="s5_params",
    )(per_state(lam_re), per_state(lam_im), per_state(ld),
      b_layout(b_re), b_layout(b_im), c_layout(c_re), c_layout(c_im))
    return ar.reshape(depth, 1, N_STATE), ai.reshape(depth, 1, N_STATE), wb, wc


def _s5_output(y, u, d_ref, wglu_ref, bglu_ref):
    y = jax.nn.gelu(y + d_ref[...] * u)
    z = _dot(y.astype(BF16), wglu_ref[...]) + bglu_ref[...]
    return (y * jax.nn.sigmoid(z)).astype(BF16)


def _s5_prompt_kernel(u_ref, ar_ref, ai_ref, wb_ref, wc_ref, d_ref, wglu_ref, bglu_ref,
                      y_ref, sre_ref, sim_ref, xre, xim, cre, cim):
    ts = S5_TIME_TILE
    nc = S5_LANE_CHUNKS

    @pl.when(pl.program_id(1) == 0)
    def _():
        cre[...] = jnp.zeros_like(cre)
        cim[...] = jnp.zeros_like(cim)

    u = u_ref[...]
    ub = u.astype(BF16)
    for r in range(S5_BLOCKS):
        x = _dot(ub[:, r * S5_BLOCK_IN:(r + 1) * S5_BLOCK_IN], wb_ref[r])
        for c in range(nc):
            xre[c, pl.ds(r, ts, stride=S5_BLOCKS), :] = x[:, c * LANES:(c + 1) * LANES]
            xim[c, pl.ds(r, ts, stride=S5_BLOCKS), :] = x[:, S5_BLOCK_STATE + c * LANES:
                                                          S5_BLOCK_STATE + (c + 1) * LANES]

    a_r = [ar_ref[c] for c in range(nc)]
    a_i = [ai_ref[c] for c in range(nc)]

    def step(t, carry):
        srs, sis = carry
        row = pl.multiple_of(t * S5_BLOCKS, S5_BLOCKS)
        new_r, new_i = [], []
        for c in range(nc):
            rows = pl.ds(row, S5_BLOCKS)
            nr = a_r[c] * srs[c] - a_i[c] * sis[c] + xre[c, rows, :]
            ni = a_r[c] * sis[c] + a_i[c] * srs[c] + xim[c, rows, :]
            xre[c, rows, :] = nr
            xim[c, rows, :] = ni
            new_r.append(nr)
            new_i.append(ni)
        return tuple(new_r), tuple(new_i)

    carry0 = (tuple(cre[c] for c in range(nc)), tuple(cim[c] for c in range(nc)))
    srs, sis = lax.fori_loop(0, ts, step, carry0, unroll=4)
    for c in range(nc):
        cre[c] = srs[c]
        cim[c] = sis[c]

    cols = []
    for r in range(S5_BLOCKS):
        sr = jnp.concatenate([xre[c, pl.ds(r, ts, stride=S5_BLOCKS), :] for c in range(nc)], axis=1)
        si = jnp.concatenate([xim[c, pl.ds(r, ts, stride=S5_BLOCKS), :] for c in range(nc)], axis=1)
        cols.append(_dot(sr.astype(BF16), wc_ref[r, 0:S5_BLOCK_STATE, :])
                    + _dot(si.astype(BF16), wc_ref[r, S5_BLOCK_STATE:, :]))
    y_ref[...] = _s5_output(jnp.concatenate(cols, axis=1), u, d_ref, wglu_ref, bglu_ref)
    sre_ref[...] = cre[...]
    sim_ref[...] = cim[...]


def _to_slabs(a):
    lead = a.shape[:-1]
    a = a.reshape(*lead, S5_BLOCKS, S5_LANE_CHUNKS, LANES)
    return jnp.swapaxes(a, -3, -2)


def _from_slabs(a):
    lead = a.shape[:-3]
    return jnp.swapaxes(a, -3, -2).reshape(*lead, N_STATE)


def _s5_prompt(u, ar, ai, wb, wc, d, w_glu_bf16, b_glu, batch, seq):
    ts = S5_TIME_TILE
    nt = seq // ts
    slab = (S5_LANE_CHUNKS, S5_BLOCKS, LANES)
    full = lambda shape: pl.BlockSpec(shape, lambda b, t: (0,) * len(shape))
    y, sre, sim = pl.pallas_call(
        _s5_prompt_kernel,
        grid=(batch, nt),
        in_specs=[
            pl.BlockSpec((ts, D_SSM), lambda b, t: (b * nt + t, 0)),
            full(slab), full(slab),
            full(wb.shape), full(wc.shape),
            full((1, D_SSM)), full((D_SSM, D_SSM)), full((1, D_SSM)),
        ],
        out_specs=[
            pl.BlockSpec((ts, D_SSM), lambda b, t: (b * nt + t, 0)),
            pl.BlockSpec((None,) + slab, lambda b, t: (b, 0, 0, 0)),
            pl.BlockSpec((None,) + slab, lambda b, t: (b, 0, 0, 0)),
        ],
        out_shape=[jax.ShapeDtypeStruct((batch * seq, D_SSM), BF16),
                   jax.ShapeDtypeStruct((batch,) + slab, F32),
                   jax.ShapeDtypeStruct((batch,) + slab, F32)],
        scratch_shapes=[pltpu.VMEM((S5_LANE_CHUNKS, ts * S5_BLOCKS, LANES), F32),
                        pltpu.VMEM((S5_LANE_CHUNKS, ts * S5_BLOCKS, LANES), F32),
                        pltpu.VMEM(slab, F32), pltpu.VMEM(slab, F32)],
        compiler_params=_params(("parallel", "arbitrary")),
        name="s5_prompt",
    )(u, _to_slabs(ar[0]), _to_slabs(ai[0]), wb, wc, d, w_glu_bf16, b_glu)
    return y, _from_slabs(sre), _from_slabs(sim)


def _s5_sample_kernel(u_ref, s0r_ref, s0i_ref, ar_ref, ai_ref, wb_ref, wc_ref, d_ref,
                      wglu_ref, bglu_ref, y_ref, sre_ref, sim_ref, yscr):
    steps = u_ref.shape[0]
    for r in range(S5_BLOCKS):
        ls = slice(r * S5_BLOCK_STATE, (r + 1) * S5_BLOCK_STATE)
        cs = slice(r * S5_BLOCK_IN, (r + 1) * S5_BLOCK_IN)
        a_r = ar_ref[:, ls]
        a_i = ai_ref[:, ls]
        sr = s0r_ref[:, ls]
        si = s0i_ref[:, ls]
        for t in range(steps):
            x = _dot(u_ref[t, :, cs].astype(BF16), wb_ref[r])
            sr, si = (a_r * sr - a_i * si + x[:, :S5_BLOCK_STATE],
                      a_r * si + a_i * sr + x[:, S5_BLOCK_STATE:])
            yscr[t, :, cs] = (_dot(sr.astype(BF16), wc_ref[r, 0:S5_BLOCK_STATE, :])
                              + _dot(si.astype(BF16), wc_ref[r, S5_BLOCK_STATE:, :]))
        sre_ref[:, ls] = sr
        sim_ref[:, ls] = si
    for t in range(steps):
        y_ref[t] = _s5_output(yscr[t], u_ref[t], d_ref, wglu_ref, bglu_ref)


def _s5_sample(u_tm, s0r, s0i, ar, ai, wb, wc, d, w_glu_bf16, b_glu):
    steps, batch, _ = u_tm.shape
    return pl.pallas_call(
        _s5_sample_kernel,
        out_shape=[jax.ShapeDtypeStruct((steps, batch, D_SSM), BF16),
                   jax.ShapeDtypeStruct((batch, N_STATE), F32),
                   jax.ShapeDtypeStruct((batch, N_STATE), F32)],
        scratch_shapes=[pltpu.VMEM((steps, batch, D_SSM), F32)],
        compiler_params=pltpu.CompilerParams(vmem_limit_bytes=VMEM_LIMIT),
        name="s5_sample",
    )(u_tm, s0r, s0i, ar, ai, wb, wc, d, w_glu_bf16, b_glu)


def _layer_out_kernel(*refs, moe, yc_slabs):
    if moe:
        (yc_ref, ys_ref, wo_ref, x_ref, gt_ref, g_ref, sh_ref, sc_ref, wr_ref, br_ref,
         xo_ref, h_ref, lg_ref) = refs
    else:
        (yc_ref, ys_ref, wo_ref, x_ref, gt_ref, g_ref, sh_ref, sc_ref, xo_ref, h_ref) = refs
    tm = x_ref.shape[0]
    if yc_slabs:
        yc = jnp.concatenate([yc_ref[pl.ds(c, tm, stride=CONV_SUB), :] for c in range(CONV_SUB)],
                             axis=1).astype(BF16)
    else:
        yc = yc_ref[...]
    o = _dot(yc, wo_ref[0:D_CONV, :]) + _dot(ys_ref[...], wo_ref[D_CONV:, :])
    x = x_ref[...] + gt_ref[...] * o
    xo_ref[...] = x
    h = _rmsnorm(x, g_ref[...]) * (1 + sc_ref[...]) + sh_ref[...]
    if moe:
        h_ref[...] = h
        h_hi = h.astype(BF16)
        h_lo = (h - h_hi.astype(F32)).astype(BF16)
        wr = wr_ref[...]
        w_hi = wr.astype(BF16)
        w_lo = (wr - w_hi.astype(F32)).astype(BF16)
        lg_ref[...] = (_dot(h_hi, w_hi) + (_dot(h_lo, w_hi) + _dot(h_hi, w_lo))) + br_ref[...]
    else:
        h_ref[...] = h.astype(BF16)


def _layer_out(grp, layer, yc, ys, w_out_bf16, x, g_norm, router=None):
    tm = 256
    moe = router is not None
    row = lambda i: (i, 0)
    const = lambda i: (0, 0)
    if grp.v_slabs:
        yc = yc.reshape(grp.rows * CONV_SUB, LANES)
        yc_spec = pl.BlockSpec((tm * CONV_SUB, LANES), row)
    else:
        yc_spec = pl.BlockSpec((tm, D_CONV), row)
    in_specs = [
        yc_spec, pl.BlockSpec((tm, D_SSM), row),
        pl.BlockSpec((D_MODEL, D_MODEL), const),
        pl.BlockSpec((tm, D_MODEL), row),
        grp.mod_spec(layer, GATE_M, tm),
        pl.BlockSpec((1, D_MODEL), const),
        grp.mod_spec(layer, SHIFT_F, tm),
        grp.mod_spec(layer, SCALE_F, tm),
    ]
    args = [yc, ys, w_out_bf16, x, grp.mod, g_norm, grp.mod, grp.mod]
    out_specs = [pl.BlockSpec((tm, D_MODEL), row), pl.BlockSpec((tm, D_MODEL), row)]
    out_shape = [jax.ShapeDtypeStruct((grp.rows, D_MODEL), F32),
                 jax.ShapeDtypeStruct((grp.rows, D_MODEL), F32 if moe else BF16)]
    if moe:
        in_specs += [pl.BlockSpec((D_MODEL, LANES), const), pl.BlockSpec((1, LANES), const)]
        args += list(router)
        out_specs.append(pl.BlockSpec((tm, LANES), row))
        out_shape.append(jax.ShapeDtypeStruct((grp.rows, LANES), F32))
    return pl.pallas_call(
        functools.partial(_layer_out_kernel, moe=moe, yc_slabs=grp.v_slabs),
        grid=(grp.rows // tm,),
        in_specs=in_specs,
        out_specs=out_specs,
        out_shape=out_shape,
        compiler_params=_params(("parallel",)),
        name="layer_out",
    )(*args)


def _ffn_kernel(h_ref, wg_ref, wu_ref, wd_ref, o_ref):
    j = pl.program_id(1)
    h = h_ref[...]
    act = (jax.nn.silu(_dot(h, wg_ref[...])) * _dot(h, wu_ref[...])).astype(BF16)
    contrib = _dot(act, wd_ref[...])

    @pl.when(j == 0)
    def _():
        o_ref[...] = contrib

    @pl.when(j > 0)
    def _():
        o_ref[...] += contrib


def _ffn_dense(h, wg, wu, wd):
    rows = h.shape[0]
    tm = min(rows, FFN_ROW_TILE)
    tf = FFN_FF_TILE
    d_ff = wg.shape[1]
    return pl.pallas_call(
        _ffn_kernel,
        grid=(rows // tm, d_ff // tf),
        in_specs=[pl.BlockSpec((tm, D_MODEL), lambda i, j: (i, 0)),
                  pl.BlockSpec((D_MODEL, tf), lambda i, j: (0, j)),
                  pl.BlockSpec((D_MODEL, tf), lambda i, j: (0, j)),
                  pl.BlockSpec((tf, D_MODEL), lambda i, j: (j, 0))],
        out_specs=pl.BlockSpec((tm, D_MODEL), lambda i, j: (i, 0)),
        out_shape=jax.ShapeDtypeStruct((rows, D_MODEL), F32),
        compiler_params=_params(("parallel", "arbitrary")),
        name="ffn_dense",
    )(h, wg, wu, wd)


R_E1, R_E2, R_P1, R_P2, R_RANK1, R_RANK2 = range(6)


def _route_kernel(lp_ref, ls_ref, meta_ref, cnt_ref, carry, *, n_first):
    i = pl.program_id(0)
    tm = meta_ref.shape[0]

    @pl.when(i == 0)
    def _():
        carry[...] = jnp.zeros_like(carry)

    lane = lax.broadcasted_iota(jnp.int32, (tm, LANES), 1).astype(F32)
    logits = jnp.where(i < n_first, lp_ref[...], ls_ref[...])
    logits = jnp.where(lane < N_EXPERTS, logits, -jnp.inf)
    m1 = jnp.max(logits, axis=-1, keepdims=True)
    e1 = jnp.min(jnp.where(logits == m1, lane, float(LANES)), axis=-1, keepdims=True)
    rest = jnp.where(lane == e1, -jnp.inf, logits)
    m2 = jnp.max(rest, axis=-1, keepdims=True)
    e2 = jnp.min(jnp.where(rest == m2, lane, float(LANES)), axis=-1, keepdims=True)
    x2 = jnp.exp(m2 - m1)
    den = 1.0 + x2
    p1 = 1.0 / den
    p2 = x2 / den

    hot1 = lane == e1
    hot2 = lane == e2
    hot = jnp.logical_or(hot1, hot2).astype(F32)
    rows = lax.broadcasted_iota(jnp.int32, (tm, tm), 0)
    cols = lax.broadcasted_iota(jnp.int32, (tm, tm), 1)
    earlier = (cols < rows).astype(BF16)
    rank = _dot(earlier, hot.astype(BF16)) + carry[0:1, :]
    rank1 = jnp.sum(jnp.where(hot1, rank, 0.0), axis=-1, keepdims=True)
    rank2 = jnp.sum(jnp.where(hot2, rank, 0.0), axis=-1, keepdims=True)
    total = carry[0:1, :] + jnp.sum(hot, axis=0, keepdims=True)
    carry[...] = jnp.broadcast_to(total, carry.shape)
    cnt_ref[...] = jnp.broadcast_to(total, cnt_ref.shape)

    meta = jnp.where(lane == R_E1, e1.astype(F32), 0.0)
    meta = jnp.where(lane == R_E2, e2.astype(F32), meta)
    meta = jnp.where(lane == R_P1, p1, meta)
    meta = jnp.where(lane == R_P2, p2, meta)
    meta = jnp.where(lane == R_RANK1, rank1, meta)
    meta = jnp.where(lane == R_RANK2, rank2, meta)
    meta_ref[...] = meta


def _route(lg_p, lg_s):
    tm = ROW_TILE
    n_first = lg_p.shape[0] // tm
    n_tiles = n_first + lg_s.shape[0] // tm
    first = lambda i: (jnp.minimum(i, n_first - 1), 0)
    second = lambda i: (jnp.maximum(i - n_first, 0), 0)
    return pl.pallas_call(
        functools.partial(_route_kernel, n_first=n_first),
        grid=(n_tiles,),
        in_specs=[pl.BlockSpec((tm, LANES), first), pl.BlockSpec((tm, LANES), second)],
        out_specs=[pl.BlockSpec((tm, LANES), lambda i: (i, 0)),
                   pl.BlockSpec((8, LANES), lambda i: (0, 0))],
        out_shape=[jax.ShapeDtypeStruct((n_tiles * tm, LANES), F32),
                   jax.ShapeDtypeStruct((8, LANES), F32)],
        scratch_shapes=[pltpu.VMEM((8, LANES), F32)],
        compiler_params=_params(("arbitrary",)),
        name="moe_route",
    )(lg_p, lg_s)


def _row_copies(n, start_fn):
    def issue(r, c):
        for cp in start_fn(r):
            cp.start()
        return c
    lax.fori_loop(0, n, issue, 0, unroll=4)

    def drain(r, c):
        for cp in start_fn(0):
            cp.wait()
        return c
    lax.fori_loop(0, n, drain, 0, unroll=4)


def _dispatch_kernel(p1_ref, p2_ref, hp_ref, hs_ref, xs_in_ref, xs_ref, sem, *, n_first):
    del xs_in_ref
    i = pl.program_id(0)
    tm = hp_ref.shape[0]
    base = i * tm

    def run(src_ref):
        def copies(r):
            row = src_ref.at[pl.ds(r, 1), :]
            return (pltpu.make_async_copy(row, xs_ref.at[pl.ds(p1_ref[base + r], 1), :], sem.at[0]),
                    pltpu.make_async_copy(row, xs_ref.at[pl.ds(p2_ref[base + r], 1), :], sem.at[1]))
        _row_copies(tm, copies)

    @pl.when(i < n_first)
    def _():
        run(hp_ref)

    @pl.when(i >= n_first)
    def _():
        run(hs_ref)


def _dispatch(pos1, pos2, h_p, h_s, n_slots):
    tm = ROW_TILE
    n_first = h_p.shape[0] // tm
    n_tiles = n_first + h_s.shape[0] // tm
    first = lambda i, p1, p2: (jnp.minimum(i, n_first - 1), 0)
    second = lambda i, p1, p2: (jnp.maximum(i - n_first, 0), 0)
    xs0 = jnp.zeros((n_slots, D_MODEL), F32)
    return pl.pallas_call(
        functools.partial(_dispatch_kernel, n_first=n_first),
        grid_spec=pltpu.PrefetchScalarGridSpec(
            num_scalar_prefetch=2,
            grid=(n_tiles,),
            in_specs=[pl.BlockSpec((tm, D_MODEL), first), pl.BlockSpec((tm, D_MODEL), second),
                      pl.BlockSpec(memory_space=pl.ANY)],
            out_specs=pl.BlockSpec(memory_space=pl.ANY),
            scratch_shapes=[pltpu.SemaphoreType.DMA((2,))],
        ),
        out_shape=jax.ShapeDtypeStruct((n_slots, D_MODEL), F32),
        input_output_aliases={4: 0},
        compiler_params=_params(("arbitrary",)),
        name="moe_dispatch",
    )(pos1, pos2, h_p, h_s, xs0)


def _moe_kernel(te_ref, nv_ref, xs_ref, wg_ref, wu_ref, wd_ref, o_ref, h_scr):
    i = pl.program_id(0)
    j = pl.program_id(1)
    n_sub = (nv_ref[i] + (MOE_SUB_ROWS - 1)) // MOE_SUB_ROWS

    @pl.when(j == 0)
    def _():
        o_ref[...] = jnp.zeros_like(o_ref)
        h_scr[...] = xs_ref[...].astype(BF16)

    for m in range(1, MOE_ROW_TILE // MOE_SUB_ROWS + 1):
        @pl.when(n_sub == m)
        def _(m=m):
            rows = slice(0, m * MOE_SUB_ROWS)
            h = h_scr[rows, :]
            act = (jax.nn.silu(_dot(h, wg_ref[...].astype(BF16)))
                   * _dot(h, wu_ref[...].astype(BF16))).astype(BF16)
            o_ref[rows, :] += _dot(act, wd_ref[...].astype(BF16))


def _moe_experts(tile_expert, n_valid, xs, wg, wu, wd):
    tm, tf = MOE_ROW_TILE, MOE_FF_TILE
    n_tiles = xs.shape[0] // tm
    d_ff = wg.shape[2]
    nj = d_ff // tf

    def ff(i, j, te, nv):
        return jnp.where(nv[i] > 0, j, nj - 1)

    return pl.pallas_call(
        _moe_kernel,
        grid_spec=pltpu.PrefetchScalarGridSpec(
            num_scalar_prefetch=2,
            grid=(n_tiles, nj),
            in_specs=[
                pl.BlockSpec((tm, D_MODEL), lambda i, j, te, nu: (i, 0)),
                pl.BlockSpec((None, D_MODEL, tf), lambda i, j, te, nu: (te[i], 0, ff(i, j, te, nu))),
                pl.BlockSpec((None, D_MODEL, tf), lambda i, j, te, nu: (te[i], 0, ff(i, j, te, nu))),
                pl.BlockSpec((None, tf, D_MODEL), lambda i, j, te, nu: (te[i], ff(i, j, te, nu), 0)),
            ],
            out_specs=pl.BlockSpec((tm, D_MODEL), lambda i, j, te, nu: (i, 0)),
            scratch_shapes=[pltpu.VMEM((tm, D_MODEL), BF16)],
        ),
        out_shape=jax.ShapeDtypeStruct(xs.shape, F32),
        compiler_params=_params(("arbitrary", "arbitrary")),
        name="moe_experts",
    )(tile_expert, n_valid, xs, wg, wu, wd)


def _combine_kernel(p1_ref, p2_ref, ys_ref, meta_ref, o_ref, y1, y2, sem):
    tm = o_ref.shape[0]
    base = pl.program_id(0) * tm

    def copies(r):
        return (pltpu.make_async_copy(ys_ref.at[pl.ds(p1_ref[base + r], 1), :],
                                      y1.at[pl.ds(r, 1), :], sem.at[0]),
                pltpu.make_async_copy(ys_ref.at[pl.ds(p2_ref[base + r], 1), :],
                                      y2.at[pl.ds(r, 1), :], sem.at[1]))
    _row_copies(tm, copies)
    meta = meta_ref[...]
    o_ref[...] = meta[:, R_P1:R_P1 + 1] * y1[...] + meta[:, R_P2:R_P2 + 1] * y2[...]


def _combine(pos1, pos2, ys, meta):
    tm = ROW_TILE
    n_rows = meta.shape[0]
    return pl.pallas_call(
        _combine_kernel,
        grid_spec=pltpu.PrefetchScalarGridSpec(
            num_scalar_prefetch=2,
            grid=(n_rows // tm,),
            in_specs=[pl.BlockSpec(memory_space=pl.ANY),
                      pl.BlockSpec((tm, LANES), lambda i, p1, p2: (i, 0))],
            out_specs=pl.BlockSpec((tm, D_MODEL), lambda i, p1, p2: (i, 0)),
            scratch_shapes=[pltpu.VMEM((tm, D_MODEL), F32), pltpu.VMEM((tm, D_MODEL), F32),
                            pltpu.SemaphoreType.DMA((2,))],
        ),
        out_shape=jax.ShapeDtypeStruct((n_rows, D_MODEL), F32),
        compiler_params=_params(("arbitrary",)),
        name="moe_combine",
    )(pos1, pos2, ys, meta)


def _moe(h_p, h_s, lg_p, lg_s, wg, wu, wd):
    tm = MOE_ROW_TILE
    n_rows = h_p.shape[0] + h_s.shape[0]
    max_tiles = (2 * n_rows) // tm + N_EXPERTS
    meta, cnt = _route(lg_p, lg_s)
    counts = cnt[0, :N_EXPERTS].astype(jnp.int32)
    tiles = (counts + tm - 1) // tm
    ends = jnp.cumsum(tiles)
    first_tile = ends - tiles
    per_tile = jnp.maximum((counts + tiles - 1) // jnp.maximum(tiles, 1), 1)

    def slot(e, rank):
        t = rank // per_tile[e]
        return (first_tile[e] + t) * tm + (rank - t * per_tile[e])

    pos1 = slot(meta[:, R_E1].astype(jnp.int32), meta[:, R_RANK1].astype(jnp.int32))
    pos2 = slot(meta[:, R_E2].astype(jnp.int32), meta[:, R_RANK2].astype(jnp.int32))
    tile_ids = jnp.arange(max_tiles, dtype=jnp.int32)
    used = tile_ids < ends[-1]
    tile_expert = jnp.sum(jnp.minimum(tile_ids, ends[-1] - 1)[:, None] >= ends[None, :],
                          axis=1).astype(jnp.int32)
    left = counts[tile_expert] - (tile_ids - first_tile[tile_expert]) * per_tile[tile_expert]
    n_valid = jnp.where(used, jnp.clip(left, 0, per_tile[tile_expert]), 0).astype(jnp.int32)

    xs = _dispatch(pos1, pos2, h_p, h_s, max_tiles * tm)
    ys = _moe_experts(tile_expert, n_valid, xs, wg, wu, wd)
    return _combine(pos1, pos2, ys, meta)


def _final_kernel(x_ref, f_ref, gt_ref, g_ref, o_ref):
    x = x_ref[...] + gt_ref[...] * f_ref[...]
    o_ref[...] = _rmsnorm(x, g_ref[...])


def _final(grp, layer, x, f_prev, g_final):
    tm = ROW_TILE
    f, off = f_prev
    return pl.pallas_call(
        _final_kernel,
        grid=(grp.rows // tm,),
        in_specs=[pl.BlockSpec((tm, D_MODEL), lambda i: (i, 0)),
                  pl.BlockSpec((tm, D_MODEL), lambda i: (i + off, 0)),
                  grp.mod_spec(layer, GATE_F, tm),
                  pl.BlockSpec((1, D_MODEL), lambda i: (0, 0))],
        out_specs=pl.BlockSpec((tm, D_MODEL), lambda i: (i, 0)),
        out_shape=jax.ShapeDtypeStruct((grp.rows, D_MODEL), F32),
        compiler_params=_params(("parallel",)),
        name="final_norm",
    )(x, f, grp.mod, g_final)


def kernel(x_prompt, x_sample, c_prompt, c_sample, state_conv, state_ssm_re, state_ssm_im, w_ada, b_ada, g_norm_mix, g_norm_ffn, w_in, w_dw, b_dw, g_ln_conv, b_ln_conv, lam_re, lam_im, log_dt, b_ssm_re, b_ssm_im, c_ssm_re, c_ssm_im, d_ssm, w_glu, b_glu, w_out, w_gate_dense, w_up_dense, w_down_dense, w_router, b_router, w_gate_exp, w_up_exp, w_down_exp, g_final):
    depth = w_in.shape[0]
    bp, seq, _ = x_prompt.shape
    bs, steps, _ = x_sample.shape
    rows_p, rows_s = bp * seq, bs * steps
    assert rows_p % ROW_TILE == 0 and rows_s == ROW_TILE and seq % ROW_TILE == 0

    n_c = bp + bs
    pad = (-n_c) % 8
    c_all = jnp.concatenate([c_prompt, c_sample, jnp.zeros((pad, D_MODEL), F32)], axis=0)
    mod = _ada(c_all, w_ada, b_ada)
    mod_p = mod[:, :bp].reshape(depth, bp, 1, 6 * D_MODEL)
    mod_s = jnp.tile(mod[:, bp:n_c], (1, steps, 1))
    grp_p = _Group(rows_p, seq, mod_p, 0, True)
    grp_s = _Group(rows_s, None, mod_s, rows_p // ROW_TILE, False)

    ar, ai, wb, wc = _s5_params(lam_re, lam_im, log_dt, b_ssm_re, b_ssm_im, c_ssm_re, c_ssm_im)

    w_in_b = [_cast_bf16(w_in, l, 512) for l in range(depth)]
    w_glu_b = [_cast_bf16(w_glu, l, 512) for l in range(depth)]
    w_out_b = [_cast_bf16(w_out, l, 512) for l in range(depth)]
    n_dense = w_gate_dense.shape[0]
    wg_d = [_cast_bf16(w_gate_dense, i, 256) for i in range(n_dense)]
    wu_d = [_cast_bf16(w_up_dense, i, 256) for i in range(n_dense)]
    wd_d = [_cast_bf16(w_down_dense, i, 512) for i in range(n_dense)]

    x_p = x_prompt.reshape(rows_p, D_MODEL)
    x_s = jnp.transpose(x_sample, (1, 0, 2)).reshape(rows_s, D_MODEL)
    conv_tm = jnp.transpose(state_conv, (0, 2, 1, 3))
    row1 = lambda a: a.reshape(1, -1)

    f_p = f_s = None
    new_conv_p, new_re_p, new_im_p, new_conv_s, new_re_s, new_im_s = [], [], [], [], [], []
    for l in range(depth):
        g_mix, g_ffn = row1(g_norm_mix[l]), row1(g_norm_ffn[l])
        conv_w = (w_dw[l], row1(b_dw[l]), row1(g_ln_conv[l]), row1(b_ln_conv[l]))
        s5_w = (ar[l], ai[l], wb[l], wc[l], row1(d_ssm[l]), w_glu_b[l], row1(b_glu[l]))

        x_p, v_p, u_p = _layer_in(grp_p, l, x_p, f_p, g_mix, w_in_b[l])
        x_s, v_s, u_s = _layer_in(grp_s, l, x_s, f_s, g_mix, w_in_b[l])

        yc_p, cst_p = _conv_prompt(v_p, *conv_w, bp, seq)
        yc_s, cst_s = _conv_sample(v_s.reshape(steps, bs, D_CONV), conv_tm[l], *conv_w)
        ys_p, sre_p, sim_p = _s5_prompt(u_p, *s5_w, bp, seq)
        ys_s, sre_s, sim_s = _s5_sample(u_s.reshape(steps, bs, D_SSM),
                                        state_ssm_re[l].reshape(bs, N_STATE),
                                        state_ssm_im[l].reshape(bs, N_STATE), *s5_w)
        new_conv_p.append(cst_p)
        new_re_p.append(sre_p.reshape(bp, N_SSM_GROUPS, SSM_STATE))
        new_im_p.append(sim_p.reshape(bp, N_SSM_GROUPS, SSM_STATE))
        new_conv_s.append(jnp.transpose(cst_s, (1, 0, 2)))
        new_re_s.append(sre_s.reshape(bs, N_SSM_GROUPS, SSM_STATE))
        new_im_s.append(sim_s.reshape(bs, N_SSM_GROUPS, SSM_STATE))

        yc_s = yc_s.reshape(rows_s, D_CONV)
        ys_s = ys_s.reshape(rows_s, D_SSM)
        i = l // 2
        if l % 2 == 0:
            x_p, h_p = _layer_out(grp_p, l, yc_p, ys_p, w_out_b[l], x_p, g_ffn)
            x_s, h_s = _layer_out(grp_s, l, yc_s, ys_s, w_out_b[l], x_s, g_ffn)
            f_p = (_ffn_dense(h_p, wg_d[i], wu_d[i], wd_d[i]), 0)
            f_s = (_ffn_dense(h_s, wg_d[i], wu_d[i], wd_d[i]), 0)
        else:
            router = (jnp.pad(w_router[i], ((0, 0), (0, LANES - N_EXPERTS))),
                      jnp.pad(b_router[i], (0, LANES - N_EXPERTS)).reshape(1, LANES))
            x_p, h_p, lg_p = _layer_out(grp_p, l, yc_p, ys_p, w_out_b[l], x_p, g_ffn, router)
            x_s, h_s, lg_s = _layer_out(grp_s, l, yc_s, ys_s, w_out_b[l], x_s, g_ffn, router)
            f = _moe(h_p, h_s, lg_p, lg_s, w_gate_exp[i], w_up_exp[i], w_down_exp[i])
            f_p, f_s = (f, grp_p.row_block_offset), (f, grp_s.row_block_offset)

    g_fin = row1(g_final)
    y_p = _final(grp_p, depth - 1, x_p, f_p, g_fin).reshape(bp, seq, D_MODEL)
    y_s = _final(grp_s, depth - 1, x_s, f_s, g_fin).reshape(steps, bs, D_MODEL)
    y_s = jnp.transpose(y_s, (1, 0, 2))
    return (y_p, y_s, jnp.stack(new_conv_p), jnp.stack(new_re_p), jnp.stack(new_im_p),
            jnp.stack(new_conv_s), jnp.stack(new_re_s), jnp.stack(new_im_s))
```

```python
import functools

import jax
import jax.numpy as jnp
from jax import lax
from jax.experimental import pallas as pl
from jax.experimental.pallas import tpu as pltpu

F32 = jnp.float32
BF16 = jnp.bfloat16

D_MODEL = 2048
D_CONV = 1024
D_SSM = 1024
CONV_WIDTH = 31
CONV_BUF = CONV_WIDTH - 1
SSM_GROUP = 16
N_SSM_GROUPS = 64
SSM_STATE = 64
LOG2_SSM_GROUP = 4
LOG2_SSM_STATE = 6
N_STATE = N_SSM_GROUPS * SSM_STATE
N_EXPERTS = 8
EPS = 1e-6

S5_BLOCKS = 8
S5_BLOCK_IN = D_SSM // S5_BLOCKS
S5_BLOCK_STATE = N_STATE // S5_BLOCKS
S5_LANE_CHUNKS = S5_BLOCK_STATE // 128
SUBLANES = 8
CONV_SUB = D_CONV // 128

LANES = 128
VMEM_LIMIT = 56 * 1024 * 1024

ROW_TILE = 512
MOE_ROW_TILE = 1152
MOE_VMEM_LIMIT = 62 * 1024 * 1024
MOE_SUB_ROWS = 128
MOE_FF_TILE = 256
FFN_ROW_TILE = 1024
FFN_FF_TILE = 512
CONV_TIME_TILE = 256
CONV_TIME_CHUNK = 16
S5_TIME_TILE = 256

SHIFT_M, SCALE_M, GATE_M, SHIFT_F, SCALE_F, GATE_F = range(6)


def _params(semantics, vmem=VMEM_LIMIT):
    return pltpu.CompilerParams(dimension_semantics=semantics, vmem_limit_bytes=vmem)


def _dot(a, b):
    return jnp.dot(a, b, preferred_element_type=F32)


def _rmsnorm(x, g):
    return x * lax.rsqrt(jnp.mean(x * x, axis=-1, keepdims=True) + EPS) * g


def _layernorm(x, g, b):
    xc = x - jnp.mean(x, axis=-1, keepdims=True)
    var = jnp.mean(xc * xc, axis=-1, keepdims=True)
    return xc * lax.rsqrt(var + EPS) * g + b


def _cast_kernel(x_ref, o_ref):
    o_ref[...] = x_ref[...].astype(o_ref.dtype)


def _cast_bf16(x, layer, rows):
    _, r, c = x.shape
    return pl.pallas_call(
        _cast_kernel,
        grid=(r // rows,),
        in_specs=[pl.BlockSpec((None, rows, c), lambda i: (layer, i, 0))],
        out_specs=pl.BlockSpec((rows, c), lambda i: (i, 0)),
        out_shape=jax.ShapeDtypeStruct((r, c), BF16),
        compiler_params=_params(("parallel",)),
        name="cast_bf16",
    )(x)


def _ada_kernel(c_ref, w_ref, b_ref, o_ref):
    cs = jax.nn.silu(c_ref[...]).astype(BF16)
    o_ref[...] = _dot(cs, w_ref[...].astype(BF16)) + b_ref[...]


def _ada(c_all, w_ada, b_ada):
    depth, _, n = w_ada.shape
    rows = c_all.shape[0]
    tn = 1024
    return pl.pallas_call(
        _ada_kernel,
        grid=(depth, n // tn),
        in_specs=[
            pl.BlockSpec((rows, D_MODEL), lambda l, j: (0, 0)),
            pl.BlockSpec((None, D_MODEL, tn), lambda l, j: (l, 0, j)),
            pl.BlockSpec((None, 1, tn), lambda l, j: (l, 0, j)),
        ],
        out_specs=pl.BlockSpec((None, rows, tn), lambda l, j: (l, 0, j)),
        out_shape=jax.ShapeDtypeStruct((depth, rows, n), F32),
        compiler_params=_params(("parallel", "parallel")),
        name="ada_mod",
    )(c_all, w_ada, b_ada.reshape(depth, 1, n))


class _Group:
    def __init__(self, rows, rows_per_batch, mod, row_block_offset, v_slabs):
        self.rows = rows
        self.rows_per_batch = rows_per_batch
        self.mod = mod
        self.row_block_offset = row_block_offset
        self.v_slabs = v_slabs

    def mod_spec(self, layer, chunk, tm):
        if self.rows_per_batch is None:
            def imap(*ids):
                return (layer, ids[0], chunk)
            return pl.BlockSpec((None, tm, D_MODEL), imap)
        per = self.rows_per_batch // tm

        def imap(*ids):
            return (layer, ids[0] // per, 0, chunk)
        return pl.BlockSpec((None, None, 1, D_MODEL), imap)


def _layer_in_kernel(*refs, has_prev, v_slabs):
    if has_prev:
        (x_ref, f_ref, gtf_ref, g_ref, sh_ref, sc_ref, w_ref, xo_ref, v_ref, u_ref) = refs
    else:
        (x_ref, g_ref, sh_ref, sc_ref, w_ref, v_ref, u_ref) = refs
    tm = x_ref.shape[0]
    x = x_ref[...]
    if has_prev:
        x = x + gtf_ref[...] * f_ref[...]
        xo_ref[...] = x
    h = (_rmsnorm(x, g_ref[...]) * (1 + sc_ref[...]) + sh_ref[...]).astype(BF16)
    tn = 512
    for j in range(D_CONV // tn):
        cols = slice(j * tn, (j + 1) * tn)
        a = _dot(h, w_ref[:, cols])
        g = _dot(h, w_ref[:, D_CONV + j * tn:D_CONV + (j + 1) * tn])
        v = a * jax.nn.sigmoid(g)
        if v_slabs:
            for c in range(tn // LANES):
                v_ref[pl.ds(j * (tn // LANES) + c, tm, stride=CONV_SUB), :] = v[:, c * LANES:(c + 1) * LANES]
        else:
            v_ref[:, cols] = v
        u_ref[:, cols] = _dot(h, w_ref[:, 2 * D_CONV + j * tn:2 * D_CONV + (j + 1) * tn])


def _layer_in(grp, layer, x, f_prev, g_norm, w_in_bf16):
    tm = 256
    has_prev = f_prev is not None
    row = lambda i: (i, 0)
    const = lambda i: (0, 0)
    in_specs = [pl.BlockSpec((tm, D_MODEL), row)]
    args = [x]
    if has_prev:
        f_arr, f_off = f_prev
        off = f_off * (ROW_TILE // tm)
        in_specs += [pl.BlockSpec((tm, D_MODEL), lambda i: (i + off, 0)),
                     grp.mod_spec(layer - 1, GATE_F, tm)]
        args += [f_arr, grp.mod]
    in_specs += [
        pl.BlockSpec((1, D_MODEL), const),
        grp.mod_spec(layer, SHIFT_M, tm),
        grp.mod_spec(layer, SCALE_M, tm),
        pl.BlockSpec(w_in_bf16.shape, const),
    ]
    args += [g_norm, grp.mod, grp.mod, w_in_bf16]
    if grp.v_slabs:
        v_spec = pl.BlockSpec((tm * CONV_SUB, LANES), row)
        v_shape = jax.ShapeDtypeStruct((grp.rows * CONV_SUB, LANES), F32)
    else:
        v_spec = pl.BlockSpec((tm, D_CONV), row)
        v_shape = jax.ShapeDtypeStruct((grp.rows, D_CONV), F32)
    out_specs = [v_spec, pl.BlockSpec((tm, D_SSM), row)]
    out_shape = [v_shape, jax.ShapeDtypeStruct((grp.rows, D_SSM), F32)]
    if has_prev:
        out_specs = [pl.BlockSpec((tm, D_MODEL), row)] + out_specs
        out_shape = [jax.ShapeDtypeStruct((grp.rows, D_MODEL), F32)] + out_shape
    outs = pl.pallas_call(
        functools.partial(_layer_in_kernel, has_prev=has_prev, v_slabs=grp.v_slabs),
        grid=(grp.rows // tm,),
        in_specs=in_specs,
        out_specs=out_specs,
        out_shape=out_shape,
        compiler_params=_params(("parallel",)),
        name="layer_in",
    )(*args)
    outs = list(outs) if has_prev else [x] + list(outs)
    if grp.v_slabs:
        outs[1] = outs[1].reshape(grp.rows, CONV_SUB, LANES)
    return outs


def _conv_epilogue(acc, b_ref, g_ref, be_ref):
    y = _layernorm(acc + b_ref[...], g_ref[...], be_ref[...])
    return jax.nn.silu(y).astype(BF16)


def _conv_prompt_kernel(v_ref, w_ref, b_ref, g_ref, be_ref, y_ref, st_ref, buf):
    tt = CONV_TIME_TILE
    tc = CONV_TIME_CHUNK
    t = pl.program_id(1)

    @pl.when(t == 0)
    def _():
        buf[0:32] = jnp.zeros((32, CONV_SUB, LANES), F32)

    @pl.when(t > 0)
    def _():
        buf[0:32] = buf[tt:tt + 32]

    buf[32:32 + tt] = v_ref[...]

    def chunk(ci, carry):
        t0 = ci * tc
        acc = None
        for k in range(CONV_WIDTH):
            term = w_ref[k] * buf[pl.ds(t0 + 2 + k, tc)]
            acc = term if acc is None else acc + term
        y = acc + b_ref[...]

        def mean_c(a):
            return jnp.sum(jnp.sum(a, axis=2, keepdims=True), axis=1, keepdims=True) * (1.0 / D_CONV)
        yc = y - mean_c(y)
        yn = yc * lax.rsqrt(mean_c(yc * yc) + EPS) * g_ref[...] + be_ref[...]
        y_ref[pl.ds(t0, tc)] = jax.nn.silu(yn)
        return carry

    lax.fori_loop(0, tt // tc, chunk, 0, unroll=2)
    st_ref[...] = buf[32 + tt - CONV_BUF:32 + tt]


def _conv_prompt(v, w_dw, b_dw, ln_g, ln_b, batch, seq):
    tt = CONV_TIME_TILE
    nt = seq // tt
    tile = (CONV_SUB, LANES)
    vec = pl.BlockSpec(tile, lambda b, t: (0, 0))
    y, st = pl.pallas_call(
        _conv_prompt_kernel,
        grid=(batch, nt),
        in_specs=[
            pl.BlockSpec((tt,) + tile, lambda b, t: (b * nt + t, 0, 0)),
            pl.BlockSpec((CONV_WIDTH,) + tile, lambda b, t: (0, 0, 0)),
            vec, vec, vec,
        ],
        out_specs=[
            pl.BlockSpec((tt,) + tile, lambda b, t: (b * nt + t, 0, 0)),
            pl.BlockSpec((None, CONV_BUF) + tile, lambda b, t: (b, 0, 0, 0)),
        ],
        out_shape=[jax.ShapeDtypeStruct((batch * seq,) + tile, F32),
                   jax.ShapeDtypeStruct((batch, CONV_BUF) + tile, F32)],
        scratch_shapes=[pltpu.VMEM((32 + tt,) + tile, F32)],
        compiler_params=_params(("parallel", "arbitrary")),
        name="conv_prompt",
    )(v, w_dw.reshape((CONV_WIDTH,) + tile), b_dw.reshape(tile), ln_g.reshape(tile), ln_b.reshape(tile))
    return y, st.reshape(batch, CONV_BUF, D_CONV)


def _conv_sample_kernel(v_ref, st_ref, w_ref, b_ref, g_ref, be_ref, y_ref, nst_ref):
    steps = v_ref.shape[0]
    for t in range(steps):
        acc = None
        for k in range(CONV_WIDTH):
            j = t + k
            src = st_ref[j] if j < CONV_BUF else v_ref[j - CONV_BUF]
            term = w_ref[k:k + 1, :] * src
            acc = term if acc is None else acc + term
        y_ref[t] = _conv_epilogue(acc, b_ref, g_ref, be_ref)
    for j in range(CONV_BUF - steps):
        nst_ref[j] = st_ref[j + steps]
    for j in range(steps):
        nst_ref[CONV_BUF - steps + j] = v_ref[j]


def _conv_sample(v_tm, state_tm, w_dw, b_dw, ln_g, ln_b):
    steps, batch, _ = v_tm.shape
    bb = 32
    vec = pl.BlockSpec((1, D_CONV), lambda i: (0, 0))
    return pl.pallas_call(
        _conv_sample_kernel,
        grid=(batch // bb,),
        in_specs=[
            pl.BlockSpec((steps, bb, D_CONV), lambda i: (0, i, 0)),
            pl.BlockSpec((CONV_BUF, bb, D_CONV), lambda i: (0, i, 0)),
            pl.BlockSpec((CONV_WIDTH, D_CONV), lambda i: (0, 0)),
            vec, vec, vec,
        ],
        out_specs=[
            pl.BlockSpec((steps, bb, D_CONV), lambda i: (0, i, 0)),
            pl.BlockSpec((CONV_BUF, bb, D_CONV), lambda i: (0, i, 0)),
        ],
        out_shape=[jax.ShapeDtypeStruct((steps, batch, D_CONV), BF16),
                   jax.ShapeDtypeStruct((CONV_BUF, batch, D_CONV), F32)],
        compiler_params=_params(("parallel",)),
        name="conv_sample",
    )(v_tm, state_tm, w_dw, b_dw, ln_g, ln_b)


def _s5_params_kernel(lr_ref, li_ref, ld_ref, br_ref, bi_ref, cr_ref, ci_ref,
                      ar_ref, ai_ref, wb_ref, wc_ref):
    lr = lr_ref[...]
    li = li_ref[...]
    dt = jnp.exp(ld_ref[...])
    mag = jnp.exp(lr * dt)
    ar = mag * jnp.cos(li * dt)
    ai = mag * jnp.sin(li * dt)
    den = lr * lr + li * li
    cr = ((ar - 1) * lr + ai * li) / den
    ci = (ai * lr - (ar - 1) * li) / den
    ar_ref[...] = ar
    ai_ref[...] = ai

    br = br_ref[...]
    bi = bi_ref[...]
    shape_b = (S5_BLOCK_IN, S5_BLOCK_STATE)
    same_b = (jnp.right_shift(lax.broadcasted_iota(jnp.int32, shape_b, 0), LOG2_SSM_GROUP)
              == jnp.right_shift(lax.broadcasted_iota(jnp.int32, shape_b, 1), LOG2_SSM_STATE))
    wb_ref[:, 0:S5_BLOCK_STATE] = jnp.where(same_b, cr * br - ci * bi, 0.0).astype(BF16)
    wb_ref[:, S5_BLOCK_STATE:] = jnp.where(same_b, cr * bi + ci * br, 0.0).astype(BF16)

    shape_c = (S5_BLOCK_STATE, S5_BLOCK_IN)
    same_c = (jnp.right_shift(lax.broadcasted_iota(jnp.int32, shape_c, 0), LOG2_SSM_STATE)
              == jnp.right_shift(lax.broadcasted_iota(jnp.int32, shape_c, 1), LOG2_SSM_GROUP))
    wc_ref[0:S5_BLOCK_STATE, :] = jnp.where(same_c, cr_ref[...], 0.0).astype(BF16)
    wc_ref[S5_BLOCK_STATE:, :] = jnp.where(same_c, -ci_ref[...], 0.0).astype(BF16)


def _s5_params(lam_re, lam_im, log_dt, b_re, b_im, c_re, c_im):
    depth = lam_re.shape[0]
    nb, gpb = S5_BLOCKS, N_SSM_GROUPS // S5_BLOCKS

    def per_state(a):
        return a.reshape(depth, nb, 1, S5_BLOCK_STATE)

    ld = jnp.broadcast_to(log_dt[:, :, None], lam_re.shape)

    def b_layout(b):
        bt = jnp.transpose(b, (0, 1, 3, 2)).reshape(depth, nb, S5_BLOCK_IN, SSM_STATE)
        return jnp.tile(bt, (1, 1, 1, gpb))

    def c_layout(c):
        ct = c.reshape(depth, nb, gpb, SSM_GROUP, SSM_STATE)
        ct = jnp.transpose(ct, (0, 1, 4, 2, 3)).reshape(depth, nb, SSM_STATE, S5_BLOCK_IN)
        return jnp.tile(ct, (1, 1, gpb, 1))

    st = pl.BlockSpec((None, None, 1, S5_BLOCK_STATE), lambda l, r: (l, r, 0, 0))
    bs = pl.BlockSpec((None, None, S5_BLOCK_IN, S5_BLOCK_STATE), lambda l, r: (l, r, 0, 0))
    cs = pl.BlockSpec((None, None, S5_BLOCK_STATE, S5_BLOCK_IN), lambda l, r: (l, r, 0, 0))
    ar, ai, wb, wc = pl.pallas_call(
        _s5_params_kernel,
        grid=(depth, nb),
        in_specs=[st, st, st, bs, bs, cs, cs],
        out_specs=[
            st, st,
            pl.BlockSpec((None, None, S5_BLOCK_IN, 2 * S5_BLOCK_STATE), lambda l, r: (l, r, 0, 0)),
            pl.BlockSpec((None, None, 2 * S5_BLOCK_STATE, S5_BLOCK_IN), lambda l, r: (l, r, 0, 0)),
        ],
        out_shape=[
            jax.ShapeDtypeStruct((depth, nb, 1, S5_BLOCK_STATE), F32),
            jax.ShapeDtypeStruct((depth, nb, 1, S5_BLOCK_STATE), F32),
            jax.ShapeDtypeStruct((depth, nb, S5_BLOCK_IN, 2 * S5_BLOCK_STATE), BF16),
            jax.ShapeDtypeStruct((depth, nb, 2 * S5_BLOCK_STATE, S5_BLOCK_IN), BF16),
        ],
        compiler_params=_params(("parallel", "parallel")),
        name="s5_params",
    )(per_state(lam_re), per_state(lam_im), per_state(ld),
      b_layout(b_re), b_layout(b_im), c_layout(c_re), c_layout(c_im))
    return ar.reshape(depth, 1, N_STATE), ai.reshape(depth, 1, N_STATE), wb, wc


def _s5_output(y, u, d_ref, wglu_ref, bglu_ref):
    y = jax.nn.gelu(y + d_ref[...] * u)
    z = _dot(y.astype(BF16), wglu_ref[...]) + bglu_ref[...]
    return (y * jax.nn.sigmoid(z)).astype(BF16)


def _s5_prompt_kernel(u_ref, ar_ref, ai_ref, wb_ref, wc_ref, d_ref, wglu_ref, bglu_ref,
                      y_ref, sre_ref, sim_ref, xre, xim, cre, cim):
    ts = S5_TIME_TILE
    nc = S5_LANE_CHUNKS

    @pl.when(pl.program_id(1) == 0)
    def _():
        cre[...] = jnp.zeros_like(cre)
        cim[...] = jnp.zeros_like(cim)

    u = u_ref[...]
    ub = u.astype(BF16)
    for r in range(S5_BLOCKS):
        x = _dot(ub[:, r * S5_BLOCK_IN:(r + 1) * S5_BLOCK_IN], wb_ref[r])
        for c in range(nc):
            xre[c, pl.ds(r, ts, stride=S5_BLOCKS), :] = x[:, c * LANES:(c + 1) * LANES]
            xim[c, pl.ds(r, ts, stride=S5_BLOCKS), :] = x[:, S5_BLOCK_STATE + c * LANES:
                                                          S5_BLOCK_STATE + (c + 1) * LANES]

    a_r = [ar_ref[c] for c in range(nc)]
    a_i = [ai_ref[c] for c in range(nc)]

    def step(t, carry):
        srs, sis = carry
        row = pl.multiple_of(t * S5_BLOCKS, S5_BLOCKS)
        new_r, new_i = [], []
        for c in range(nc):
            rows = pl.ds(row, S5_BLOCKS)
            nr = a_r[c] * srs[c] - a_i[c] * sis[c] + xre[c, rows, :]
            ni = a_r[c] * sis[c] + a_i[c] * srs[c] + xim[c, rows, :]
            xre[c, rows, :] = nr
            xim[c, rows, :] = ni
            new_r.append(nr)
            new_i.append(ni)
        return tuple(new_r), tuple(new_i)

    carry0 = (tuple(cre[c] for c in range(nc)), tuple(cim[c] for c in range(nc)))
    srs, sis = lax.fori_loop(0, ts, step, carry0, unroll=4)
    for c in range(nc):
        cre[c] = srs[c]
        cim[c] = sis[c]

    cols = []
    for r in range(S5_BLOCKS):
        sr = jnp.concatenate([xre[c, pl.ds(r, ts, stride=S5_BLOCKS), :] for c in range(nc)], axis=1)
        si = jnp.concatenate([xim[c, pl.ds(r, ts, stride=S5_BLOCKS), :] for c in range(nc)], axis=1)
        cols.append(_dot(sr.astype(BF16), wc_ref[r, 0:S5_BLOCK_STATE, :])
                    + _dot(si.astype(BF16), wc_ref[r, S5_BLOCK_STATE:, :]))
    y_ref[...] = _s5_output(jnp.concatenate(cols, axis=1), u, d_ref, wglu_ref, bglu_ref)
    sre_ref[...] = cre[...]
    sim_ref[...] = cim[...]


def _to_slabs(a):
    lead = a.shape[:-1]
    a = a.reshape(*lead, S5_BLOCKS, S5_LANE_CHUNKS, LANES)
    return jnp.swapaxes(a, -3, -2)


def _from_slabs(a):
    lead = a.shape[:-3]
    return jnp.swapaxes(a, -3, -2).reshape(*lead, N_STATE)


def _s5_prompt(u, ar, ai, wb, wc, d, w_glu_bf16, b_glu, batch, seq):
    ts = S5_TIME_TILE
    nt = seq // ts
    slab = (S5_LANE_CHUNKS, S5_BLOCKS, LANES)
    full = lambda shape: pl.BlockSpec(shape, lambda b, t: (0,) * len(shape))
    y, sre, sim = pl.pallas_call(
        _s5_prompt_kernel,
        grid=(batch, nt),
        in_specs=[
            pl.BlockSpec((ts, D_SSM), lambda b, t: (b * nt + t, 0)),
            full(slab), full(slab),
            full(wb.shape), full(wc.shape),
            full((1, D_SSM)), full((D_SSM, D_SSM)), full((1, D_SSM)),
        ],
        out_specs=[
            pl.BlockSpec((ts, D_SSM), lambda b, t: (b * nt + t, 0)),
            pl.BlockSpec((None,) + slab, lambda b, t: (b, 0, 0, 0)),
            pl.BlockSpec((None,) + slab, lambda b, t: (b, 0, 0, 0)),
        ],
        out_shape=[jax.ShapeDtypeStruct((batch * seq, D_SSM), BF16),
                   jax.ShapeDtypeStruct((batch,) + slab, F32),
                   jax.ShapeDtypeStruct((batch,) + slab, F32)],
        scratch_shapes=[pltpu.VMEM((S5_LANE_CHUNKS, ts * S5_BLOCKS, LANES), F32),
                        pltpu.VMEM((S5_LANE_CHUNKS, ts * S5_BLOCKS, LANES), F32),
                        pltpu.VMEM(slab, F32), pltpu.VMEM(slab, F32)],
        compiler_params=_params(("parallel", "arbitrary")),
        name="s5_prompt",
    )(u, _to_slabs(ar[0]), _to_slabs(ai[0]), wb, wc, d, w_glu_bf16, b_glu)
    return y, _from_slabs(sre), _from_slabs(sim)


def _s5_sample_kernel(u_ref, s0r_ref, s0i_ref, ar_ref, ai_ref, wb_ref, wc_ref, d_ref,
                      wglu_ref, bglu_ref, y_ref, sre_ref, sim_ref, yscr):
    steps = u_ref.shape[0]
    for r in range(S5_BLOCKS):
        ls = slice(r * S5_BLOCK_STATE, (r + 1) * S5_BLOCK_STATE)
        cs = slice(r * S5_BLOCK_IN, (r + 1) * S5_BLOCK_IN)
        a_r = ar_ref[:, ls]
        a_i = ai_ref[:, ls]
        sr = s0r_ref[:, ls]
        si = s0i_ref[:, ls]
        for t in range(steps):
            x = _dot(u_ref[t, :, cs].astype(BF16), wb_ref[r])
            sr, si = (a_r * sr - a_i * si + x[:, :S5_BLOCK_STATE],
                      a_r * si + a_i * sr + x[:, S5_BLOCK_STATE:])
            yscr[t, :, cs] = (_dot(sr.astype(BF16), wc_ref[r, 0:S5_BLOCK_STATE, :])
                              + _dot(si.astype(BF16), wc_ref[r, S5_BLOCK_STATE:, :]))
        sre_ref[:, ls] = sr
        sim_ref[:, ls] = si
    for t in range(steps):
        y_ref[t] = _s5_output(yscr[t], u_ref[t], d_ref, wglu_ref, bglu_ref)


def _s5_sample(u_tm, s0r, s0i, ar, ai, wb, wc, d, w_glu_bf16, b_glu):
    steps, batch, _ = u_tm.shape
    return pl.pallas_call(
        _s5_sample_kernel,
        out_shape=[jax.ShapeDtypeStruct((steps, batch, D_SSM), BF16),
                   jax.ShapeDtypeStruct((batch, N_STATE), F32),
                   jax.ShapeDtypeStruct((batch, N_STATE), F32)],
        scratch_shapes=[pltpu.VMEM((steps, batch, D_SSM), F32)],
        compiler_params=pltpu.CompilerParams(vmem_limit_bytes=VMEM_LIMIT),
        name="s5_sample",
    )(u_tm, s0r, s0i, ar, ai, wb, wc, d, w_glu_bf16, b_glu)


def _layer_out_kernel(*refs, moe, yc_slabs):
    if moe:
        (yc_ref, ys_ref, wo_ref, x_ref, gt_ref, g_ref, sh_ref, sc_ref, wr_ref, br_ref,
         xo_ref, h_ref, lg_ref) = refs
    else:
        (yc_ref, ys_ref, wo_ref, x_ref, gt_ref, g_ref, sh_ref, sc_ref, xo_ref, h_ref) = refs
    tm = x_ref.shape[0]
    if yc_slabs:
        yc = jnp.concatenate([yc_ref[pl.ds(c, tm, stride=CONV_SUB), :] for c in range(CONV_SUB)],
                             axis=1).astype(BF16)
    else:
        yc = yc_ref[...]
    o = _dot(yc, wo_ref[0:D_CONV, :]) + _dot(ys_ref[...], wo_ref[D_CONV:, :])
    x = x_ref[...] + gt_ref[...] * o
    xo_ref[...] = x
    h = _rmsnorm(x, g_ref[...]) * (1 + sc_ref[...]) + sh_ref[...]
    if moe:
        h_ref[...] = h
        h_hi = h.astype(BF16)
        h_lo = (h - h_hi.astype(F32)).astype(BF16)
        wr = wr_ref[...]
        w_hi = wr.astype(BF16)
        w_lo = (wr - w_hi.astype(F32)).astype(BF16)
        lg_ref[...] = (_dot(h_hi, w_hi) + (_dot(h_lo, w_hi) + _dot(h_hi, w_lo))) + br_ref[...]
    else:
        h_ref[...] = h.astype(BF16)


def _layer_out(grp, layer, yc, ys, w_out_bf16, x, g_norm, router=None):
    tm = 256
    moe = router is not None
    row = lambda i: (i, 0)
    const = lambda i: (0, 0)
    if grp.v_slabs:
        yc = yc.reshape(grp.rows * CONV_SUB, LANES)
        yc_spec = pl.BlockSpec((tm * CONV_SUB, LANES), row)
    else:
        yc_spec = pl.BlockSpec((tm, D_CONV), row)
    in_specs = [
        yc_spec, pl.BlockSpec((tm, D_SSM), row),
        pl.BlockSpec((D_MODEL, D_MODEL), const),
        pl.BlockSpec((tm, D_MODEL), row),
        grp.mod_spec(layer, GATE_M, tm),
        pl.BlockSpec((1, D_MODEL), const),
        grp.mod_spec(layer, SHIFT_F, tm),
        grp.mod_spec(layer, SCALE_F, tm),
    ]
    args = [yc, ys, w_out_bf16, x, grp.mod, g_norm, grp.mod, grp.mod]
    out_specs = [pl.BlockSpec((tm, D_MODEL), row), pl.BlockSpec((tm, D_MODEL), row)]
    out_shape = [jax.ShapeDtypeStruct((grp.rows, D_MODEL), F32),
                 jax.ShapeDtypeStruct((grp.rows, D_MODEL), F32 if moe else BF16)]
    if moe:
        in_specs += [pl.BlockSpec((D_MODEL, LANES), const), pl.BlockSpec((1, LANES), const)]
        args += list(router)
        out_specs.append(pl.BlockSpec((tm, LANES), row))
        out_shape.append(jax.ShapeDtypeStruct((grp.rows, LANES), F32))
    return pl.pallas_call(
        functools.partial(_layer_out_kernel, moe=moe, yc_slabs=grp.v_slabs),
        grid=(grp.rows // tm,),
        in_specs=in_specs,
        out_specs=out_specs,
        out_shape=out_shape,
        compiler_params=_params(("parallel",)),
        name="layer_out",
    )(*args)


def _ffn_kernel(h_ref, wg_ref, wu_ref, wd_ref, o_ref):
    j = pl.program_id(1)
    h = h_ref[...]
    act = (jax.nn.silu(_dot(h, wg_ref[...])) * _dot(h, wu_ref[...])).astype(BF16)
    contrib = _dot(act, wd_ref[...])

    @pl.when(j == 0)
    def _():
        o_ref[...] = contrib

    @pl.when(j > 0)
    def _():
        o_ref[...] += contrib


def _ffn_dense(h, wg, wu, wd):
    rows = h.shape[0]
    tm = min(rows, FFN_ROW_TILE)
    tf = FFN_FF_TILE
    d_ff = wg.shape[1]
    return pl.pallas_call(
        _ffn_kernel,
        grid=(rows // tm, d_ff // tf),
        in_specs=[pl.BlockSpec((tm, D_MODEL), lambda i, j: (i, 0)),
                  pl.BlockSpec((D_MODEL, tf), lambda i, j: (0, j)),
                  pl.BlockSpec((D_MODEL, tf), lambda i, j: (0, j)),
                  pl.BlockSpec((tf, D_MODEL), lambda i, j: (j, 0))],
        out_specs=pl.BlockSpec((tm, D_MODEL), lambda i, j: (i, 0)),
        out_shape=jax.ShapeDtypeStruct((rows, D_MODEL), F32),
        compiler_params=_params(("parallel", "arbitrary")),
        name="ffn_dense",
    )(h, wg, wu, wd)


R_E1, R_E2, R_P1, R_P2, R_RANK1, R_RANK2 = range(6)


def _route_kernel(lp_ref, ls_ref, meta_ref, cnt_ref, carry, *, n_first):
    i = pl.program_id(0)
    tm = meta_ref.shape[0]

    @pl.when(i == 0)
    def _():
        carry[...] = jnp.zeros_like(carry)

    lane = lax.broadcasted_iota(jnp.int32, (tm, LANES), 1).astype(F32)
    logits = jnp.where(i < n_first, lp_ref[...], ls_ref[...])
    logits = jnp.where(lane < N_EXPERTS, logits, -jnp.inf)
    m1 = jnp.max(logits, axis=-1, keepdims=True)
    e1 = jnp.min(jnp.where(logits == m1, lane, float(LANES)), axis=-1, keepdims=True)
    rest = jnp.where(lane == e1, -jnp.inf, logits)
    m2 = jnp.max(rest, axis=-1, keepdims=True)
    e2 = jnp.min(jnp.where(rest == m2, lane, float(LANES)), axis=-1, keepdims=True)
    x2 = jnp.exp(m2 - m1)
    den = 1.0 + x2
    p1 = 1.0 / den
    p2 = x2 / den

    hot1 = lane == e1
    hot2 = lane == e2
    hot = jnp.logical_or(hot1, hot2).astype(F32)
    rows = lax.broadcasted_iota(jnp.int32, (tm, tm), 0)
    cols = lax.broadcasted_iota(jnp.int32, (tm, tm), 1)
    earlier = (cols < rows).astype(BF16)
    rank = _dot(earlier, hot.astype(BF16)) + carry[0:1, :]
    rank1 = jnp.sum(jnp.where(hot1, rank, 0.0), axis=-1, keepdims=True)
    rank2 = jnp.sum(jnp.where(hot2, rank, 0.0), axis=-1, keepdims=True)
    total = carry[0:1, :] + jnp.sum(hot, axis=0, keepdims=True)
    carry[...] = jnp.broadcast_to(total, carry.shape)
    cnt_ref[...] = jnp.broadcast_to(total, cnt_ref.shape)

    meta = jnp.where(lane == R_E1, e1.astype(F32), 0.0)
    meta = jnp.where(lane == R_E2, e2.astype(F32), meta)
    meta = jnp.where(lane == R_P1, p1, meta)
    meta = jnp.where(lane == R_P2, p2, meta)
    meta = jnp.where(lane == R_RANK1, rank1, meta)
    meta = jnp.where(lane == R_RANK2, rank2, meta)
    meta_ref[...] = meta


def _route(lg_p, lg_s):
    tm = ROW_TILE
    n_first = lg_p.shape[0] // tm
    n_tiles = n_first + lg_s.shape[0] // tm
    first = lambda i: (jnp.minimum(i, n_first - 1), 0)
    second = lambda i: (jnp.maximum(i - n_first, 0), 0)
    return pl.pallas_call(
        functools.partial(_route_kernel, n_first=n_first),
        grid=(n_tiles,),
        in_specs=[pl.BlockSpec((tm, LANES), first), pl.BlockSpec((tm, LANES), second)],
        out_specs=[pl.BlockSpec((tm, LANES), lambda i: (i, 0)),
                   pl.BlockSpec((8, LANES), lambda i: (0, 0))],
        out_shape=[jax.ShapeDtypeStruct((n_tiles * tm, LANES), F32),
                   jax.ShapeDtypeStruct((8, LANES), F32)],
        scratch_shapes=[pltpu.VMEM((8, LANES), F32)],
        compiler_params=_params(("arbitrary",)),
        name="moe_route",
    )(lg_p, lg_s)


def _row_copies(n, start_fn):
    def issue(r, c):
        for cp in start_fn(r):
            cp.start()
        return c
    lax.fori_loop(0, n, issue, 0, unroll=4)

    def drain(r, c):
        for cp in start_fn(0):
            cp.wait()
        return c
    lax.fori_loop(0, n, drain, 0, unroll=4)


def _dispatch_kernel(p1_ref, p2_ref, hp_ref, hs_ref, xs_in_ref, xs_ref, sem, *, n_first):
    del xs_in_ref
    i = pl.program_id(0)
    tm = hp_ref.shape[0]
    base = i * tm

    def run(src_ref):
        def copies(r):
            row = src_ref.at[pl.ds(r, 1), :]
            return (pltpu.make_async_copy(row, xs_ref.at[pl.ds(p1_ref[base + r], 1), :], sem.at[0]),
                    pltpu.make_async_copy(row, xs_ref.at[pl.ds(p2_ref[base + r], 1), :], sem.at[1]))
        _row_copies(tm, copies)

    @pl.when(i < n_first)
    def _():
        run(hp_ref)

    @pl.when(i >= n_first)
    def _():
        run(hs_ref)


def _dispatch(pos1, pos2, h_p, h_s, n_slots):
    tm = ROW_TILE
    n_first = h_p.shape[0] // tm
    n_tiles = n_first + h_s.shape[0] // tm
    first = lambda i, p1, p2: (jnp.minimum(i, n_first - 1), 0)
    second = lambda i, p1, p2: (jnp.maximum(i - n_first, 0), 0)
    xs0 = jnp.zeros((n_slots, D_MODEL), F32)
    return pl.pallas_call(
        functools.partial(_dispatch_kernel, n_first=n_first),
        grid_spec=pltpu.PrefetchScalarGridSpec(
            num_scalar_prefetch=2,
            grid=(n_tiles,),
            in_specs=[pl.BlockSpec((tm, D_MODEL), first), pl.BlockSpec((tm, D_MODEL), second),
                      pl.BlockSpec(memory_space=pl.ANY)],
            out_specs=pl.BlockSpec(memory_space=pl.ANY),
            scratch_shapes=[pltpu.SemaphoreType.DMA((2,))],
        ),
        out_shape=jax.ShapeDtypeStruct((n_slots, D_MODEL), F32),
        input_output_aliases={4: 0},
        compiler_params=_params(("arbitrary",)),
        name="moe_dispatch",
    )(pos1, pos2, h_p, h_s, xs0)


def _moe_kernel(te_ref, nv_ref, xs_ref, wg_ref, wu_ref, wd_ref, o_ref):
    i = pl.program_id(0)
    j = pl.program_id(1)
    n_sub = (nv_ref[i] + (MOE_SUB_ROWS - 1)) // MOE_SUB_ROWS

    @pl.when(j == 0)
    def _():
        o_ref[...] = jnp.zeros_like(o_ref)

    for m in range(1, MOE_ROW_TILE // MOE_SUB_ROWS + 1):
        @pl.when(n_sub == m)
        def _(m=m):
            rows = slice(0, m * MOE_SUB_ROWS)
            h = xs_ref[rows, :].astype(BF16)
            act = (jax.nn.silu(_dot(h, wg_ref[...].astype(BF16)))
                   * _dot(h, wu_ref[...].astype(BF16))).astype(BF16)
            o_ref[rows, :] += _dot(act, wd_ref[...].astype(BF16))


def _moe_experts(tile_expert, n_valid, xs, wg, wu, wd):
    tm, tf = MOE_ROW_TILE, MOE_FF_TILE
    n_tiles = xs.shape[0] // tm
    d_ff = wg.shape[2]
    nj = d_ff // tf

    def ff(i, j, te, nv):
        return jnp.where(nv[i] > 0, j, nj - 1)

    return pl.pallas_call(
        _moe_kernel,
        grid_spec=pltpu.PrefetchScalarGridSpec(
            num_scalar_prefetch=2,
            grid=(n_tiles, nj),
            in_specs=[
                pl.BlockSpec((tm, D_MODEL), lambda i, j, te, nu: (i, 0)),
                pl.BlockSpec((None, D_MODEL, tf), lambda i, j, te, nu: (te[i], 0, ff(i, j, te, nu))),
                pl.BlockSpec((None, D_MODEL, tf), lambda i, j, te, nu: (te[i], 0, ff(i, j, te, nu))),
                pl.BlockSpec((None, tf, D_MODEL), lambda i, j, te, nu: (te[i], ff(i, j, te, nu), 0)),
            ],
            out_specs=pl.BlockSpec((tm, D_MODEL), lambda i, j, te, nu: (i, 0)),
        ),
        out_shape=jax.ShapeDtypeStruct(xs.shape, F32),
        compiler_params=_params(("arbitrary", "arbitrary"), MOE_VMEM_LIMIT),
        name="moe_experts",
    )(tile_expert, n_valid, xs, wg, wu, wd)


def _combine_kernel(p1_ref, p2_ref, ys_ref, meta_ref, o_ref, y1, y2, sem):
    tm = o_ref.shape[0]
    base = pl.program_id(0) * tm

    def copies(r):
        return (pltpu.make_async_copy(ys_ref.at[pl.ds(p1_ref[base + r], 1), :],
                                      y1.at[pl.ds(r, 1), :], sem.at[0]),
                pltpu.make_async_copy(ys_ref.at[pl.ds(p2_ref[base + r], 1), :],
                                      y2.at[pl.ds(r, 1), :], sem.at[1]))
    _row_copies(tm, copies)
    meta = meta_ref[...]
    o_ref[...] = meta[:, R_P1:R_P1 + 1] * y1[...] + meta[:, R_P2:R_P2 + 1] * y2[...]


def _combine(pos1, pos2, ys, meta):
    tm = ROW_TILE
    n_rows = meta.shape[0]
    return pl.pallas_call(
        _combine_kernel,
        grid_spec=pltpu.PrefetchScalarGridSpec(
            num_scalar_prefetch=2,
            grid=(n_rows // tm,),
            in_specs=[pl.BlockSpec(memory_space=pl.ANY),
                      pl.BlockSpec((tm, LANES), lambda i, p1, p2: (i, 0))],
            out_specs=pl.BlockSpec((tm, D_MODEL), lambda i, p1, p2: (i, 0)),
            scratch_shapes=[pltpu.VMEM((tm, D_MODEL), F32), pltpu.VMEM((tm, D_MODEL), F32),
                            pltpu.SemaphoreType.DMA((2,))],
        ),
        out_shape=jax.ShapeDtypeStruct((n_rows, D_MODEL), F32),
        compiler_params=_params(("arbitrary",)),
        name="moe_combine",
    )(pos1, pos2, ys, meta)


def _moe(h_p, h_s, lg_p, lg_s, wg, wu, wd):
    tm = MOE_ROW_TILE
    n_rows = h_p.shape[0] + h_s.shape[0]
    max_tiles = (2 * n_rows) // tm + N_EXPERTS
    meta, cnt = _route(lg_p, lg_s)
    counts = cnt[0, :N_EXPERTS].astype(jnp.int32)
    tiles = (counts + tm - 1) // tm
    ends = jnp.cumsum(tiles)
    first_tile = ends - tiles
    per_tile = jnp.maximum((counts + tiles - 1) // jnp.maximum(tiles, 1), 1)

    def slot(e, rank):
        t = rank // per_tile[e]
        return (first_tile[e] + t) * tm + (rank - t * per_tile[e])

    pos1 = slot(meta[:, R_E1].astype(jnp.int32), meta[:, R_RANK1].astype(jnp.int32))
    pos2 = slot(meta[:, R_E2].astype(jnp.int32), meta[:, R_RANK2].astype(jnp.int32))
    tile_ids = jnp.arange(max_tiles, dtype=jnp.int32)
    used = tile_ids < ends[-1]
    tile_expert = jnp.sum(jnp.minimum(tile_ids, ends[-1] - 1)[:, None] >= ends[None, :],
                          axis=1).astype(jnp.int32)
    left = counts[tile_expert] - (tile_ids - first_tile[tile_expert]) * per_tile[tile_expert]
    n_valid = jnp.where(used, jnp.clip(left, 0, per_tile[tile_expert]), 0).astype(jnp.int32)

    xs = _dispatch(pos1, pos2, h_p, h_s, max_tiles * tm)
    ys = _moe_experts(tile_expert, n_valid, xs, wg, wu, wd)
    return _combine(pos1, pos2, ys, meta)


def _final_kernel(x_ref, f_ref, gt_ref, g_ref, o_ref):
    x = x_ref[...] + gt_ref[...] * f_ref[...]
    o_ref[...] = _rmsnorm(x, g_ref[...])


def _final(grp, layer, x, f_prev, g_final):
    tm = ROW_TILE
    f, off = f_prev
    return pl.pallas_call(
        _final_kernel,
        grid=(grp.rows // tm,),
        in_specs=[pl.BlockSpec((tm, D_MODEL), lambda i: (i, 0)),
                  pl.BlockSpec((tm, D_MODEL), lambda i: (i + off, 0)),
                  grp.mod_spec(layer, GATE_F, tm),
                  pl.BlockSpec((1, D_MODEL), lambda i: (0, 0))],
        out_specs=pl.BlockSpec((tm, D_MODEL), lambda i: (i, 0)),
        out_shape=jax.ShapeDtypeStruct((grp.rows, D_MODEL), F32),
        compiler_params=_params(("parallel",)),
        name="final_norm",
    )(x, f, grp.mod, g_final)


def kernel(x_prompt, x_sample, c_prompt, c_sample, state_conv, state_ssm_re, state_ssm_im, w_ada, b_ada, g_norm_mix, g_norm_ffn, w_in, w_dw, b_dw, g_ln_conv, b_ln_conv, lam_re, lam_im, log_dt, b_ssm_re, b_ssm_im, c_ssm_re, c_ssm_im, d_ssm, w_glu, b_glu, w_out, w_gate_dense, w_up_dense, w_down_dense, w_router, b_router, w_gate_exp, w_up_exp, w_down_exp, g_final):
    depth = w_in.shape[0]
    bp, seq, _ = x_prompt.shape
    bs, steps, _ = x_sample.shape
    rows_p, rows_s = bp * seq, bs * steps
    assert rows_p % ROW_TILE == 0 and rows_s == ROW_TILE and seq % ROW_TILE == 0

    n_c = bp + bs
    pad = (-n_c) % 8
    c_all = jnp.concatenate([c_prompt, c_sample, jnp.zeros((pad, D_MODEL), F32)], axis=0)
    mod = _ada(c_all, w_ada, b_ada)
    mod_p = mod[:, :bp].reshape(depth, bp, 1, 6 * D_MODEL)
    mod_s = jnp.tile(mod[:, bp:n_c], (1, steps, 1))
    grp_p = _Group(rows_p, seq, mod_p, 0, True)
    grp_s = _Group(rows_s, None, mod_s, rows_p // ROW_TILE, False)

    ar, ai, wb, wc = _s5_params(lam_re, lam_im, log_dt, b_ssm_re, b_ssm_im, c_ssm_re, c_ssm_im)

    w_in_b = [_cast_bf16(w_in, l, 512) for l in range(depth)]
    w_glu_b = [_cast_bf16(w_glu, l, 512) for l in range(depth)]
    w_out_b = [_cast_bf16(w_out, l, 512) for l in range(depth)]
    n_dense = w_gate_dense.shape[0]
    wg_d = [_cast_bf16(w_gate_dense, i, 256) for i in range(n_dense)]
    wu_d = [_cast_bf16(w_up_dense, i, 256) for i in range(n_dense)]
    wd_d = [_cast_bf16(w_down_dense, i, 512) for i in range(n_dense)]

    x_p = x_prompt.reshape(rows_p, D_MODEL)
    x_s = jnp.transpose(x_sample, (1, 0, 2)).reshape(rows_s, D_MODEL)
    conv_tm = jnp.transpose(state_conv, (0, 2, 1, 3))
    row1 = lambda a: a.reshape(1, -1)

    f_p = f_s = None
    new_conv_p, new_re_p, new_im_p, new_conv_s, new_re_s, new_im_s = [], [], [], [], [], []
    for l in range(depth):
        g_mix, g_ffn = row1(g_norm_mix[l]), row1(g_norm_ffn[l])
        conv_w = (w_dw[l], row1(b_dw[l]), row1(g_ln_conv[l]), row1(b_ln_conv[l]))
        s5_w = (ar[l], ai[l], wb[l], wc[l], row1(d_ssm[l]), w_glu_b[l], row1(b_glu[l]))

        x_p, v_p, u_p = _layer_in(grp_p, l, x_p, f_p, g_mix, w_in_b[l])
        x_s, v_s, u_s = _layer_in(grp_s, l, x_s, f_s, g_mix, w_in_b[l])

        yc_p, cst_p = _conv_prompt(v_p, *conv_w, bp, seq)
        yc_s, cst_s = _conv_sample(v_s.reshape(steps, bs, D_CONV), conv_tm[l], *conv_w)
        ys_p, sre_p, sim_p = _s5_prompt(u_p, *s5_w, bp, seq)
        ys_s, sre_s, sim_s = _s5_sample(u_s.reshape(steps, bs, D_SSM),
                                        state_ssm_re[l].reshape(bs, N_STATE),
                                        state_ssm_im[l].reshape(bs, N_STATE), *s5_w)
        new_conv_p.append(cst_p)
        new_re_p.append(sre_p.reshape(bp, N_SSM_GROUPS, SSM_STATE))
        new_im_p.append(sim_p.reshape(bp, N_SSM_GROUPS, SSM_STATE))
        new_conv_s.append(jnp.transpose(cst_s, (1, 0, 2)))
        new_re_s.append(sre_s.reshape(bs, N_SSM_GROUPS, SSM_STATE))
        new_im_s.append(sim_s.reshape(bs, N_SSM_GROUPS, SSM_STATE))

        yc_s = yc_s.reshape(rows_s, D_CONV)
        ys_s = ys_s.reshape(rows_s, D_SSM)
        i = l // 2
        if l % 2 == 0:
            x_p, h_p = _layer_out(grp_p, l, yc_p, ys_p, w_out_b[l], x_p, g_ffn)
            x_s, h_s = _layer_out(grp_s, l, yc_s, ys_s, w_out_b[l], x_s, g_ffn)
            f_p = (_ffn_dense(h_p, wg_d[i], wu_d[i], wd_d[i]), 0)
            f_s = (_ffn_dense(h_s, wg_d[i], wu_d[i], wd_d[i]), 0)
        else:
            router = (jnp.pad(w_router[i], ((0, 0), (0, LANES - N_EXPERTS))),
                      jnp.pad(b_router[i], (0, LANES - N_EXPERTS)).reshape(1, LANES))
            x_p, h_p, lg_p = _layer_out(grp_p, l, yc_p, ys_p, w_out_b[l], x_p, g_ffn, router)
            x_s, h_s, lg_s = _layer_out(grp_s, l, yc_s, ys_s, w_out_b[l], x_s, g_ffn, router)
            f = _moe(h_p, h_s, lg_p, lg_s, w_gate_exp[i], w_up_exp[i], w_down_exp[i])
            f_p, f_s = (f, grp_p.row_block_offset), (f, grp_s.row_block_offset)

    g_fin = row1(g_final)
    y_p = _final(grp_p, depth - 1, x_p, f_p, g_fin).reshape(bp, seq, D_MODEL)
    y_s = _final(grp_s, depth - 1, x_s, f_s, g_fin).reshape(steps, bs, D_MODEL)
    y_s = jnp.transpose(y_s, (1, 0, 2))
    return (y_p, y_s, jnp.stack(new_conv_p), jnp.stack(new_re_p), jnp.stack(new_im_p),
            jnp.stack(new_conv_s), jnp.stack(new_re_s), jnp.stack(new_im_s))
```
